```python
import math
import jax, jax.numpy as jnp
from jax import lax
import numpy as np

D_MODEL = 1024
BATCH = 8
SEQ = 8192
DEPTH = 1

CHUNK = 64
Q_BLOCK = 128
CONV_WIDTH = D_MODEL // 2
ATTN_WIDTH = D_MODEL - CONV_WIDTH
N_DIFF_HEADS = 4
DIFF_HEAD_DIM = ATTN_WIDTH // (2 * N_DIFF_HEADS)
CONV_KERNEL = 31
FFN_DIM = 2816
FFN_CONV_KERNEL = 3
ROPE_THETA = 10000.0
RMS_EPS = 1e-6
LN_EPS = 1e-5
SUBLN_EPS = 1e-5
IN_COLS = 2 * CONV_WIDTH + 3 * ATTN_WIDTH

kernel_name = "hymba_conformer_diffattn_convglu_block"


def rms_norm(x, g, eps=RMS_EPS):
    xf = x.astype(jnp.float32)
    y = xf * lax.rsqrt(jnp.mean(xf * xf, axis=-1, keepdims=True) + eps)
    return (y * g.astype(jnp.float32)).astype(x.dtype)


def layer_norm(x, g, b, eps=LN_EPS):
    xf = x.astype(jnp.float32)
    mu = jnp.mean(xf, axis=-1, keepdims=True)
    var = jnp.mean(jnp.square(xf - mu), axis=-1, keepdims=True)
    y = (xf - mu) * lax.rsqrt(var + eps)
    return (y * g.astype(jnp.float32) + b.astype(jnp.float32)).astype(x.dtype)


def causal_dwconv(u, w, b):
    k = w.shape[0]
    c = u.shape[-1]
    out = lax.conv_general_dilated(
        u, w[:, None, :].astype(u.dtype), window_strides=(1,), padding=[(k - 1, 0)],
        dimension_numbers=("NWC", "WIO", "NWC"), feature_group_count=c)
    return out + b.astype(u.dtype)


def rope_tables(seq, dim):
    inv_freq = 1.0 / (ROPE_THETA ** (jnp.arange(0, dim, 2, dtype=jnp.float32) / dim))
    ang = jnp.arange(seq, dtype=jnp.float32)[:, None] * inv_freq[None, :]
    return jnp.cos(ang), jnp.sin(ang)


def apply_rope(t, cos, sin):
    tf = t.astype(jnp.float32)
    half = tf.shape[-1] // 2
    t1, t2 = tf[..., :half], tf[..., half:]
    c = cos[None, :, None, None, :]
    s = sin[None, :, None, None, :]
    out = jnp.concatenate([t1 * c - t2 * s, t2 * c + t1 * s], axis=-1)
    return out.astype(t.dtype)


def diff_attention(q, k, v, lam):
    b, s, h, _, dh = q.shape
    nblk = s // Q_BLOCK
    scale = dh ** -0.5
    q_blocks = q.reshape(b, nblk, Q_BLOCK, h, 2, dh).swapaxes(0, 1)
    key_chunk = jnp.arange(s) // CHUNK

    def one_block(args):
        qb, blk = args
        q_chunk = (blk * Q_BLOCK + jnp.arange(Q_BLOCK)) // CHUNK
        mask = key_chunk[None, :] <= q_chunk[:, None]
        scores = jnp.einsum("bqhmd,bkhmd->bhmqk", qb, k).astype(jnp.float32) * scale
        scores = jnp.where(mask, scores, -jnp.inf)
        p = jax.nn.softmax(scores, axis=-1)
        w = p[:, :, 0] - lam * p[:, :, 1]
        return jnp.einsum("bhqk,bkhe->bqhe", w.astype(v.dtype), v)

    out = lax.map(one_block, (q_blocks, jnp.arange(nblk)))
    return out.swapaxes(0, 1).reshape(b, s, h, 2 * dh)


def setup_inputs(seed: int = 0) -> dict:
    key = jax.random.key(seed)
    ks = jax.random.split(key, 24)
    f32 = jnp.float32

    def nrm(k, shape, scale):
        return jax.random.normal(k, shape, f32) * scale

    dh = DIFF_HEAD_DIM
    return {
        "x": nrm(ks[0], (BATCH, SEQ, D_MODEL), 1.0),
        "g_mix": 1.0 + nrm(ks[1], (DEPTH, D_MODEL), 0.02),
        "w_in": nrm(ks[2], (DEPTH, D_MODEL, IN_COLS), D_MODEL ** -0.5),
        "w_dw": nrm(ks[3], (DEPTH, CONV_KERNEL, CONV_WIDTH), CONV_KERNEL ** -0.5),
        "b_dw": nrm(ks[4], (DEPTH, CONV_WIDTH), 0.02),
        "ln_g": 1.0 + nrm(ks[5], (DEPTH, CONV_WIDTH), 0.02),
        "ln_b": nrm(ks[6], (DEPTH, CONV_WIDTH), 0.02),
        "lambda_q1": nrm(ks[7], (DEPTH, dh), 0.1),
        "lambda_k1": nrm(ks[8], (DEPTH, dh), 0.1),
        "lambda_q2": nrm(ks[9], (DEPTH, dh), 0.1),
        "lambda_k2": nrm(ks[10], (DEPTH, dh), 0.1),
        "g_subln": 1.0 + nrm(ks[11], (DEPTH, 2 * dh), 0.02),
        "w_out": nrm(ks[12], (DEPTH, CONV_WIDTH + ATTN_WIDTH, D_MODEL), (CONV_WIDTH + ATTN_WIDTH) ** -0.5),
        "g_ffn": 1.0 + nrm(ks[13], (DEPTH, D_MODEL), 0.02),
        "w_up": nrm(ks[14], (DEPTH, D_MODEL, 2 * FFN_DIM), D_MODEL ** -0.5),
        "w_fconv": nrm(ks[15], (DEPTH, FFN_CONV_KERNEL, FFN_DIM), FFN_CONV_KERNEL ** -0.5),
        "b_fconv": nrm(ks[16], (DEPTH, FFN_DIM), 0.02),
        "w_down": nrm(ks[17], (DEPTH, FFN_DIM, D_MODEL), FFN_DIM ** -0.5),
        "g_final": 1.0 + nrm(ks[18], (D_MODEL,), 0.02),
    }


def reference(x, g_mix, w_in, w_dw, b_dw, ln_g, ln_b, lambda_q1, lambda_k1, lambda_q2,
              lambda_k2, g_subln, w_out, g_ffn, w_up, w_fconv, b_fconv, w_down, g_final):
    b, s, _ = x.shape
    cos, sin = rope_tables(s, DIFF_HEAD_DIM)
    splits = [CONV_WIDTH, 2 * CONV_WIDTH, 2 * CONV_WIDTH + ATTN_WIDTH, 2 * CONV_WIDTH + 2 * ATTN_WIDTH]
    for layer in range(DEPTH):
        h = rms_norm(x, g_mix[layer])
        proj = h @ w_in[layer]
        c_val, c_gate, q, k, v = jnp.split(proj, splits, axis=-1)

        a = c_val * jax.nn.sigmoid(c_gate)
        a = causal_dwconv(a, w_dw[layer], b_dw[layer])
        a = jax.nn.silu(layer_norm(a, ln_g[layer], ln_b[layer]))

        q = apply_rope(q.reshape(b, s, N_DIFF_HEADS, 2, DIFF_HEAD_DIM), cos, sin)
        k = apply_rope(k.reshape(b, s, N_DIFF_HEADS, 2, DIFF_HEAD_DIM), cos, sin)
        v = v.reshape(b, s, N_DIFF_HEADS, 2 * DIFF_HEAD_DIM)
        lam_init = 0.8 - 0.6 * math.exp(-0.3 * layer)
        lam = (jnp.exp(jnp.sum(lambda_q1[layer].astype(jnp.float32) * lambda_k1[layer].astype(jnp.float32)))
               - jnp.exp(jnp.sum(lambda_q2[layer].astype(jnp.float32) * lambda_k2[layer].astype(jnp.float32)))
               + lam_init)
        o = diff_attention(q, k, v, lam)
        o = rms_norm(o, g_subln[layer], SUBLN_EPS) * (1.0 - lam_init)
        o = o.reshape(b, s, ATTN_WIDTH)

        x = x + jnp.concatenate([a, o], axis=-1) @ w_out[layer]

        h = rms_norm(x, g_ffn[layer])
        gate, val = jnp.split(h @ w_up[layer], 2, axis=-1)
        gate = causal_dwconv(gate, w_fconv[layer], b_fconv[layer])
        x = x + (jax.nn.silu(gate) * val) @ w_down[layer]
    return rms_norm(x, g_final)
```

```python
import functools
import math

import jax
import jax.numpy as jnp
from jax import lax
from jax.experimental import pallas as pl
from jax.experimental.pallas import tpu as pltpu

D_MODEL = 1024
CHUNK = 64
CONV_WIDTH = 512
ATTN_WIDTH = 512
N_HEADS = 4
HEAD_DIM = 64
V_DIM = 2 * HEAD_DIM
CONV_KERNEL = 31
FFN_DIM = 2816
FFN_CONV_KERNEL = 3
ROPE_THETA = 10000.0
RMS_EPS = 1e-6
LN_EPS = 1e-5
SUBLN_EPS = 1e-5
QK_COLS = 2 * CONV_WIDTH + 2 * ATTN_WIDTH

LANES = 128
SUBLANES = 8
LOG2E = math.log2(math.e)
NEG_BIG = -1e30

MIX_ROWS = 512
CONV_ROWS = 512
CONV_HALO = 32
CONV_CHUNK = 16
ATT_Q = 256
ATT_K = 256
FFN_ROWS = 256
FFN_HALO = SUBLANES

VMEM_LIMIT = 56 * 1024 * 1024


def _nt_dot(a, b):
    return lax.dot_general(a, b, (((1,), (1,)), ((), ())), preferred_element_type=jnp.float32)


def _nn_dot(a, b):
    return jnp.dot(a, b, preferred_element_type=jnp.float32)


def _sigmoid(x):
    return 1.0 / (1.0 + jnp.exp(-x))


def _mix_in_kernel(x_ref, g_ref, w_ref, wvt_ref, cos_ref, sin_ref,
                   apre_ref, q_ref, k_ref, vt_ref, *, q_scale):
    x = x_ref[0]
    ms = jnp.mean(x * x, axis=-1, keepdims=True)
    h = (x * lax.rsqrt(ms + RMS_EPS) * g_ref[...]).astype(jnp.bfloat16)

    main = _nn_dot(h, w_ref[...])
    c_val = main[:, :CONV_WIDTH]
    c_gate = main[:, CONV_WIDTH:2 * CONV_WIDTH]
    apre_ref[0] = c_val * _sigmoid(c_gate)

    cos = cos_ref[...]
    sin = sin_ref[...]
    lane = lax.broadcasted_iota(jnp.int32, cos.shape, 1)
    first_half = (lane % HEAD_DIM) < (HEAD_DIM // 2)

    def rope(t):
        up = pltpu.roll(t, LANES - HEAD_DIM // 2, axis=1)
        down = pltpu.roll(t, HEAD_DIM // 2, axis=1)
        return t * cos + jnp.where(first_half, up, down) * sin

    q0 = 2 * CONV_WIDTH
    k0 = q0 + ATTN_WIDTH
    for hd in range(N_HEADS):
        qh = main[:, q0 + hd * V_DIM:q0 + (hd + 1) * V_DIM]
        kh = main[:, k0 + hd * V_DIM:k0 + (hd + 1) * V_DIM]
        q_ref[0, hd] = (rope(qh) * q_scale).astype(jnp.bfloat16)
        k_ref[0, hd] = rope(kh).astype(jnp.bfloat16)

    vt = _nt_dot(wvt_ref[...], h).astype(jnp.bfloat16)
    rows = x.shape[0]
    for j in range(rows // ATT_K):
        vt_ref[0, j] = vt[:, j * ATT_K:(j + 1) * ATT_K]


def _mix_in(x, g_mix, w_main, w_vt, cos_t, sin_t):
    b, s, d = x.shape
    t = MIX_ROWS
    q_scale = HEAD_DIM ** -0.5 * LOG2E
    const = lambda bi, i: (0, 0)
    return pl.pallas_call(
        functools.partial(_mix_in_kernel, q_scale=q_scale),
        grid=(b, s // t),
        in_specs=[
            pl.BlockSpec((1, t, d), lambda bi, i: (bi, i, 0)),
            pl.BlockSpec((1, d), const),
            pl.BlockSpec((d, QK_COLS), const),
            pl.BlockSpec((ATTN_WIDTH, d), const),
            pl.BlockSpec((t, LANES), lambda bi, i: (i, 0)),
            pl.BlockSpec((t, LANES), lambda bi, i: (i, 0)),
        ],
        out_specs=[
            pl.BlockSpec((1, t, CONV_WIDTH), lambda bi, i: (bi, i, 0)),
            pl.BlockSpec((1, N_HEADS, t, V_DIM), lambda bi, i: (bi, 0, i, 0)),
            pl.BlockSpec((1, N_HEADS, t, V_DIM), lambda bi, i: (bi, 0, i, 0)),
            pl.BlockSpec((1, t // ATT_K, ATTN_WIDTH, ATT_K), lambda bi, i: (bi, i, 0, 0)),
        ],
        out_shape=[
            jax.ShapeDtypeStruct((b, s, CONV_WIDTH), jnp.float32),
            jax.ShapeDtypeStruct((b, N_HEADS, s, V_DIM), jnp.bfloat16),
            jax.ShapeDtypeStruct((b, N_HEADS, s, V_DIM), jnp.bfloat16),
            jax.ShapeDtypeStruct((b, s // ATT_K, ATTN_WIDTH, ATT_K), jnp.bfloat16),
        ],
        compiler_params=pltpu.CompilerParams(
            dimension_semantics=("arbitrary", "arbitrary"), vmem_limit_bytes=VMEM_LIMIT),
        name="mix_in",
    )(x, g_mix, w_main, w_vt, cos_t, sin_t)


def _conv_mod_kernel(cur_ref, halo_ref, w_ref, b_ref, lng_ref, lnb_ref, a_ref, u_ref):
    i = pl.program_id(1)
    rows = cur_ref.shape[1]
    halo = halo_ref[0]
    u_ref[0, 0:CONV_HALO, :] = jnp.where(i == 0, jnp.zeros_like(halo), halo)
    u_ref[0, CONV_HALO:CONV_HALO + rows, :] = cur_ref[0]
    span = CONV_HALO + rows - SUBLANES
    for r in range(1, SUBLANES):
        u_ref[r, 0:span, :] = u_ref[0, r:r + span, :]

    bias = b_ref[...]
    lng = lng_ref[...]
    lnb = lnb_ref[...]
    first_tap = CONV_HALO - (CONV_KERNEL - 1)

    def chunk(c, carry):
        base = pl.multiple_of(c * CONV_CHUNK, CONV_CHUNK)
        acc = jnp.broadcast_to(bias, (CONV_CHUNK, CONV_WIDTH))
        for tap in range(CONV_KERNEL):
            off = first_tap + tap
            aligned = pl.multiple_of(base + (off // SUBLANES) * SUBLANES, SUBLANES)
            acc = acc + w_ref[tap:tap + 1, :] * u_ref[off % SUBLANES, pl.ds(aligned, CONV_CHUNK), :]
        mu = jnp.mean(acc, axis=-1, keepdims=True)
        cen = acc - mu
        var = jnp.mean(cen * cen, axis=-1, keepdims=True)
        y = cen * lax.rsqrt(var + LN_EPS) * lng + lnb
        a_ref[0, pl.ds(base, CONV_CHUNK), :] = (y * _sigmoid(y)).astype(a_ref.dtype)
        return carry

    lax.fori_loop(0, rows // CONV_CHUNK, chunk, 0)


def _conv_mod(a_pre, w_dw, b_dw, ln_g, ln_b):
    b, s, c = a_pre.shape
    t = CONV_ROWS
    per = t // CONV_HALO
    const = lambda bi, i: (0, 0)
    return pl.pallas_call(
        _conv_mod_kernel,
        grid=(b, s // t),
        in_specs=[
            pl.BlockSpec((1, t, c), lambda bi, i: (bi, i, 0)),
            pl.BlockSpec((1, CONV_HALO, c), lambda bi, i: (bi, jnp.maximum(i * per - 1, 0), 0)),
            pl.BlockSpec((CONV_HALO, c), const),
            pl.BlockSpec((1, c), const),
            pl.BlockSpec((1, c), const),
            pl.BlockSpec((1, c), const),
        ],
        out_specs=pl.BlockSpec((1, t, c), lambda bi, i: (bi, i, 0)),
        out_shape=jax.ShapeDtypeStruct((b, s, c), jnp.bfloat16),
        scratch_shapes=[pltpu.VMEM((SUBLANES, CONV_HALO + t, c), jnp.float32)],
        compiler_params=pltpu.CompilerParams(
            dimension_semantics=("arbitrary", "arbitrary"), vmem_limit_bytes=VMEM_LIMIT),
        name="conv_mod",
    )(a_pre, a_pre, w_dw, b_dw, ln_g, ln_b)


def _diff_attn_kernel(q_ref, k_ref, vt_ref, lamv_ref, g_ref, o_ref, m_ref, l_ref, acc_ref,
                      *, lam_init):
    seq = q_ref.shape[2]
    tq, tk = ATT_Q, ATT_K
    lamv = lamv_ref[...]
    lam = (jnp.exp(jnp.sum(lamv[0:1] * lamv[1:2], axis=-1, keepdims=True))
           - jnp.exp(jnp.sum(lamv[2:3] * lamv[3:4], axis=-1, keepdims=True)) + lam_init)
    gain = g_ref[...] * (1.0 - lam_init)

    lane = lax.broadcasted_iota(jnp.int32, (tq, V_DIM), 1)
    key_chunk = lax.broadcasted_iota(jnp.int32, (tk, 2 * tq), 0) // CHUNK
    qry_chunk = (lax.broadcasted_iota(jnp.int32, (tk, 2 * tq), 1) % tq) // CHUNK
    diag_visible = key_chunk <= qry_chunk

    def q_tile(qi, carry):
        q0 = pl.multiple_of(qi * tq, tq)
        qt = q_ref[0, 0, pl.ds(q0, tq), :]
        zero = jnp.zeros_like(qt)
        qq = jnp.concatenate([jnp.where(lane < HEAD_DIM, qt, zero),
                              jnp.where(lane >= HEAD_DIM, qt, zero)], axis=0)
        m_ref[...] = jnp.full(m_ref.shape, NEG_BIG, jnp.float32)
        l_ref[...] = jnp.zeros(l_ref.shape, jnp.float32)
        acc_ref[...] = jnp.zeros(acc_ref.shape, jnp.float32)

        def kv_tile(ki, masked):
            k0 = pl.multiple_of(ki * tk, tk)
            kt = k_ref[0, 0, pl.ds(k0, tk), :]
            s = _nt_dot(kt, qq)
            if masked:
                s = jnp.where(diag_visible, s, NEG_BIG)
            m_old = m_ref[...]
            m_new = jnp.maximum(m_old, jnp.max(s, axis=0, keepdims=True))
            alpha = jnp.exp2(m_old - m_new)
            p = jnp.exp2(s - m_new)
            l_ref[...] = alpha * l_ref[...] + jnp.sum(p, axis=0, keepdims=True)
            acc_ref[...] = alpha * acc_ref[...] + _nn_dot(vt_ref[0, ki], p.astype(jnp.bfloat16))
            m_ref[...] = m_new

        kv_tile(qi, True)
        lax.fori_loop(0, qi, lambda ki, c: (kv_tile(ki, False), c)[1], 0)

        acc = acc_ref[...]
        inv_l = 1.0 / l_ref[...]
        o_t = acc[:, :tq] * inv_l[:, :tq] - lam * (acc[:, tq:] * inv_l[:, tq:])
        o = o_t.T
        ms = jnp.mean(o * o, axis=-1, keepdims=True)
        o_ref[0, 0, pl.ds(q0, tq), :] = (o * lax.rsqrt(ms + SUBLN_EPS) * gain).astype(o_ref.dtype)
        return carry

    lax.fori_loop(0, seq // tq, q_tile, 0)


def _diff_attn(q, k, vt, lamv, g_subln, lam_init):
    b, nh, s, dv = q.shape
    return pl.pallas_call(
        functools.partial(_diff_attn_kernel, lam_init=lam_init),
        grid=(b, nh),
        in_specs=[
            pl.BlockSpec((1, 1, s, dv), lambda bi, hi: (bi, hi, 0, 0)),
            pl.BlockSpec((1, 1, s, dv), lambda bi, hi: (bi, hi, 0, 0)),
            pl.BlockSpec((1, s // ATT_K, dv, ATT_K), lambda bi, hi: (bi, 0, hi, 0)),
            pl.BlockSpec((SUBLANES, LANES), lambda bi, hi: (0, 0)),
            pl.BlockSpec((1, dv), lambda bi, hi: (0, 0)),
        ],
        out_specs=pl.BlockSpec((1, 1, s, dv), lambda bi, hi: (bi, hi, 0, 0)),
        out_shape=jax.ShapeDtypeStruct((b, nh, s, dv), jnp.bfloat16),
        scratch_shapes=[
            pltpu.VMEM((1, 2 * ATT_Q), jnp.float32),
            pltpu.VMEM((1, 2 * ATT_Q), jnp.float32),
            pltpu.VMEM((dv, 2 * ATT_Q), jnp.float32),
        ],
        compiler_params=pltpu.CompilerParams(
            dimension_semantics=("arbitrary", "arbitrary"), vmem_limit_bytes=VMEM_LIMIT),
        name="diff_attn",
    )(q, k, vt, lamv, g_subln)


def _rms(x, g):
    ms = jnp.mean(x * x, axis=-1, keepdims=True)
    return x * lax.rsqrt(ms + RMS_EPS) * g


def _ffn_out_kernel(x_ref, a_ref, o_ref, wout_ref, gffn_ref, wup_ref, wf_ref, bf_ref, wdn_ref,
                    gfin_ref, y_ref, gate_ref, *, final_norm):
    i = pl.program_id(1)
    rows = x_ref.shape[1]
    o_cat = jnp.concatenate([o_ref[0, hd] for hd in range(N_HEADS)], axis=-1)
    ao = jnp.concatenate([a_ref[0], o_cat], axis=-1)
    x1 = x_ref[0] + _nn_dot(ao, wout_ref[...])

    h = _rms(x1, gffn_ref[...]).astype(jnp.bfloat16)
    up = _nn_dot(h, wup_ref[...])
    val = up[:, FFN_DIM:]

    @pl.when(i == 0)
    def _():
        gate_ref[0:FFN_HALO, :] = jnp.zeros((FFN_HALO, FFN_DIM), jnp.float32)

    gate_ref[FFN_HALO:FFN_HALO + rows, :] = up[:, :FFN_DIM]
    conv = bf_ref[...]
    for tap in range(FFN_CONV_KERNEL):
        off = FFN_HALO - (FFN_CONV_KERNEL - 1) + tap
        conv = conv + wf_ref[tap:tap + 1, :] * gate_ref[off:off + rows, :]
    carry_rows = gate_ref[rows:rows + FFN_HALO, :]
    hidden = (conv * _sigmoid(conv) * val).astype(jnp.bfloat16)
    x2 = x1 + _nn_dot(hidden, wdn_ref[...])
    gate_ref[0:FFN_HALO, :] = carry_rows
    if final_norm:
        x2 = _rms(x2, gfin_ref[...])
    y_ref[0] = x2


def _ffn_out(x, a, o, w_out, g_ffn, w_up, w_fconv, b_fconv, w_down, g_final, final_norm):
    b, s, d = x.shape
    t = FFN_ROWS
    const = lambda bi, i: (0, 0)
    resident = functools.partial(pl.BlockSpec, index_map=const, pipeline_mode=pl.Buffered(1))
    return pl.pallas_call(
        functools.partial(_ffn_out_kernel, final_norm=final_norm),
        grid=(b, s // t),
        in_specs=[
            pl.BlockSpec((1, t, d), lambda bi, i: (bi, i, 0)),
            pl.BlockSpec((1, t, CONV_WIDTH), lambda bi, i: (bi, i, 0)),
            pl.BlockSpec((1, N_HEADS, t, V_DIM), lambda bi, i: (bi, 0, i, 0)),
            resident((d, d)),
            pl.BlockSpec((1, d), const),
            resident((d, 2 * FFN_DIM)),
            pl.BlockSpec((SUBLANES, FFN_DIM), const),
            pl.BlockSpec((1, FFN_DIM), const),
            resident((FFN_DIM, d)),
            pl.BlockSpec((1, d), const),
        ],
        out_specs=pl.BlockSpec((1, t, d), lambda bi, i: (bi, i, 0)),
        out_shape=jax.ShapeDtypeStruct((b, s, d), jnp.float32),
        scratch_shapes=[pltpu.VMEM((FFN_HALO + t, FFN_DIM), jnp.float32)],
        compiler_params=pltpu.CompilerParams(
            dimension_semantics=("arbitrary", "arbitrary"), vmem_limit_bytes=VMEM_LIMIT),
        name="ffn_out",
    )(x, a, o, w_out, g_ffn, w_up, w_fconv, b_fconv, w_down, g_final)


def _rope_tables(seq):
    inv_freq = 1.0 / (ROPE_THETA ** (jnp.arange(0, HEAD_DIM, 2, dtype=jnp.float32) / HEAD_DIM))
    ang = jnp.arange(seq, dtype=jnp.float32)[:, None] * inv_freq[None, :]
    cos, sin = jnp.cos(ang), jnp.sin(ang)
    reps = LANES // HEAD_DIM
    cos_t = jnp.tile(jnp.concatenate([cos, cos], axis=-1), (1, reps))
    sin_t = jnp.tile(jnp.concatenate([-sin, sin], axis=-1), (1, reps))
    return cos_t, sin_t


def _pad_rows(w, rows):
    return jnp.pad(w, ((0, rows - w.shape[0]), (0, 0)))


def kernel(x, g_mix, w_in, w_dw, b_dw, ln_g, ln_b, lambda_q1, lambda_k1, lambda_q2, lambda_k2,
           g_subln, w_out, g_ffn, w_up, w_fconv, b_fconv, w_down, g_final):
    depth = w_in.shape[0]
    seq = x.shape[1]
    cos_t, sin_t = _rope_tables(seq)
    bf = jnp.bfloat16
    row = lambda v: v.reshape(1, -1).astype(jnp.float32)
    for layer in range(depth):
        w_main = w_in[layer][:, :QK_COLS].astype(bf)
        w_vt = w_in[layer][:, QK_COLS:].T.astype(bf)
        a_pre, q, k, vt = _mix_in(x, row(g_mix[layer]), w_main, w_vt, cos_t, sin_t)
        a = _conv_mod(a_pre, _pad_rows(w_dw[layer], CONV_HALO), row(b_dw[layer]),
                      row(ln_g[layer]), row(ln_b[layer]))
        lam_init = 0.8 - 0.6 * math.exp(-0.3 * layer)
        lam_rows = jnp.stack([lambda_q1[layer], lambda_k1[layer], lambda_q2[layer], lambda_k2[layer]])
        lamv = jnp.pad(lam_rows.astype(jnp.float32),
                       ((0, SUBLANES - 4), (0, LANES - lam_rows.shape[1])))
        o = _diff_attn(q, k, vt, lamv, row(g_subln[layer]), lam_init)
        x = _ffn_out(x, a, o, w_out[layer].astype(bf), row(g_ffn[layer]), w_up[layer].astype(bf),
                     _pad_rows(w_fconv[layer], SUBLANES), row(b_fconv[layer]),
                     w_down[layer].astype(bf), row(g_final), final_norm=(layer == depth - 1))
    return x
```

```python
import functools
import math

import jax
import jax.numpy as jnp
from jax import lax
from jax.experimental import pallas as pl
from jax.experimental.pallas import tpu as pltpu

D_MODEL = 1024
CHUNK = 64
CONV_WIDTH = 512
ATTN_WIDTH = 512
N_HEADS = 4
HEAD_DIM = 64
V_DIM = 2 * HEAD_DIM
CONV_KERNEL = 31
FFN_DIM = 2816
FFN_CONV_KERNEL = 3
ROPE_THETA = 10000.0
RMS_EPS = 1e-6
LN_EPS = 1e-5
SUBLN_EPS = 1e-5
QK_COLS = 2 * CONV_WIDTH + 2 * ATTN_WIDTH

LANES = 128
SUBLANES = 8
LOG2E = math.log2(math.e)
NEG_BIG = -1e30

MIX_ROWS = 512
CONV_ROWS = 512
CONV_HALO = 32
CONV_CHUNK = 16
ATT_Q = 256
ATT_K = 256
FFN_ROWS = 256
FFN_HALO = SUBLANES

VMEM_LIMIT = 56 * 1024 * 1024


def _nt_dot(a, b):
    return lax.dot_general(a, b, (((1,), (1,)), ((), ())), preferred_element_type=jnp.float32)


def _nn_dot(a, b):
    return jnp.dot(a, b, preferred_element_type=jnp.float32)


def _sigmoid(x):
    return 1.0 / (1.0 + jnp.exp(-x))


def _mix_in_kernel(x_ref, g_ref, w_ref, wvt_ref, cos_ref, sin_ref,
                   apre_ref, q_ref, k_ref, vt_ref, *, q_scale):
    x = x_ref[0]
    ms = jnp.mean(x * x, axis=-1, keepdims=True)
    h = (x * lax.rsqrt(ms + RMS_EPS) * g_ref[...]).astype(jnp.bfloat16)

    main = _nn_dot(h, w_ref[...])
    c_val = main[:, :CONV_WIDTH]
    c_gate = main[:, CONV_WIDTH:2 * CONV_WIDTH]
    apre_ref[0] = c_val * _sigmoid(c_gate)

    cos = cos_ref[...]
    sin = sin_ref[...]
    lane = lax.broadcasted_iota(jnp.int32, cos.shape, 1)
    first_half = (lane % HEAD_DIM) < (HEAD_DIM // 2)

    def rope(t):
        up = pltpu.roll(t, LANES - HEAD_DIM // 2, axis=1)
        down = pltpu.roll(t, HEAD_DIM // 2, axis=1)
        return t * cos + jnp.where(first_half, up, down) * sin

    q0 = 2 * CONV_WIDTH
    k0 = q0 + ATTN_WIDTH
    for hd in range(N_HEADS):
        qh = main[:, q0 + hd * V_DIM:q0 + (hd + 1) * V_DIM]
        kh = main[:, k0 + hd * V_DIM:k0 + (hd + 1) * V_DIM]
        q_ref[0, hd] = (rope(qh) * q_scale).astype(jnp.bfloat16)
        k_ref[0, hd] = rope(kh).astype(jnp.bfloat16)

    vt = _nt_dot(wvt_ref[...], h).astype(jnp.bfloat16)
    rows = x.shape[0]
    for j in range(rows // ATT_K):
        vt_ref[0, j] = vt[:, j * ATT_K:(j + 1) * ATT_K]


def _mix_in(x, g_mix, w_main, w_vt, cos_t, sin_t):
    b, s, d = x.shape
    t = MIX_ROWS
    q_scale = HEAD_DIM ** -0.5 * LOG2E
    const = lambda bi, i: (0, 0)
    return pl.pallas_call(
        functools.partial(_mix_in_kernel, q_scale=q_scale),
        grid=(b, s // t),
        in_specs=[
            pl.BlockSpec((1, t, d), lambda bi, i: (bi, i, 0)),
            pl.BlockSpec((1, d), const),
            pl.BlockSpec((d, QK_COLS), const),
            pl.BlockSpec((ATTN_WIDTH, d), const),
            pl.BlockSpec((t, LANES), lambda bi, i: (i, 0)),
            pl.BlockSpec((t, LANES), lambda bi, i: (i, 0)),
        ],
        out_specs=[
            pl.BlockSpec((1, t, CONV_WIDTH), lambda bi, i: (bi, i, 0)),
            pl.BlockSpec((1, N_HEADS, t, V_DIM), lambda bi, i: (bi, 0, i, 0)),
            pl.BlockSpec((1, N_HEADS, t, V_DIM), lambda bi, i: (bi, 0, i, 0)),
            pl.BlockSpec((1, t // ATT_K, ATTN_WIDTH, ATT_K), lambda bi, i: (bi, i, 0, 0)),
        ],
        out_shape=[
            jax.ShapeDtypeStruct((b, s, CONV_WIDTH), jnp.float32),
            jax.ShapeDtypeStruct((b, N_HEADS, s, V_DIM), jnp.bfloat16),
            jax.ShapeDtypeStruct((b, N_HEADS, s, V_DIM), jnp.bfloat16),
            jax.ShapeDtypeStruct((b, s // ATT_K, ATTN_WIDTH, ATT_K), jnp.bfloat16),
        ],
        compiler_params=pltpu.CompilerParams(
            dimension_semantics=("arbitrary", "arbitrary"), vmem_limit_bytes=VMEM_LIMIT),
        name="mix_in",
    )(x, g_mix, w_main, w_vt, cos_t, sin_t)


def _conv_mod_kernel(cur_ref, halo_ref, w_ref, b_ref, lng_ref, lnb_ref, a_ref, u_ref):
    i = pl.program_id(1)
    rows = cur_ref.shape[1]
    halo = halo_ref[0]
    u_ref[0, 0:CONV_HALO, :] = jnp.where(i == 0, jnp.zeros_like(halo), halo)
    u_ref[0, CONV_HALO:CONV_HALO + rows, :] = cur_ref[0]
    span = CONV_HALO + rows - SUBLANES
    for r in range(1, SUBLANES):
        u_ref[r, 0:span, :] = u_ref[0, r:r + span, :]

    bias = b_ref[...]
    lng = lng_ref[...]
    lnb = lnb_ref[...]
    first_tap = CONV_HALO - (CONV_KERNEL - 1)

    def chunk(c, carry):
        base = pl.multiple_of(c * CONV_CHUNK, CONV_CHUNK)
        acc = jnp.broadcast_to(bias, (CONV_CHUNK, CONV_WIDTH))
        for tap in range(CONV_KERNEL):
            off = first_tap + tap
            aligned = pl.multiple_of(base + (off // SUBLANES) * SUBLANES, SUBLANES)
            acc = acc + w_ref[tap:tap + 1, :] * u_ref[off % SUBLANES, pl.ds(aligned, CONV_CHUNK), :]
        mu = jnp.mean(acc, axis=-1, keepdims=True)
        cen = acc - mu
        var = jnp.mean(cen * cen, axis=-1, keepdims=True)
        y = cen * lax.rsqrt(var + LN_EPS) * lng + lnb
        a_ref[0, pl.ds(base, CONV_CHUNK), :] = (y * _sigmoid(y)).astype(a_ref.dtype)
        return carry

    lax.fori_loop(0, rows // CONV_CHUNK, chunk, 0)


def _conv_mod(a_pre, w_dw, b_dw, ln_g, ln_b):
    b, s, c = a_pre.shape
    t = CONV_ROWS
    per = t // CONV_HALO
    const = lambda bi, i: (0, 0)
    return pl.pallas_call(
        _conv_mod_kernel,
        grid=(b, s // t),
        in_specs=[
            pl.BlockSpec((1, t, c), lambda bi, i: (bi, i, 0)),
            pl.BlockSpec((1, CONV_HALO, c), lambda bi, i: (bi, jnp.maximum(i * per - 1, 0), 0)),
            pl.BlockSpec((CONV_HALO, c), const),
            pl.BlockSpec((1, c), const),
            pl.BlockSpec((1, c), const),
            pl.BlockSpec((1, c), const),
        ],
        out_specs=pl.BlockSpec((1, t, c), lambda bi, i: (bi, i, 0)),
        out_shape=jax.ShapeDtypeStruct((b, s, c), jnp.bfloat16),
        scratch_shapes=[pltpu.VMEM((SUBLANES, CONV_HALO + t, c), jnp.float32)],
        compiler_params=pltpu.CompilerParams(
            dimension_semantics=("arbitrary", "arbitrary"), vmem_limit_bytes=VMEM_LIMIT),
        name="conv_mod",
    )(a_pre, a_pre, w_dw, b_dw, ln_g, ln_b)


def _diff_attn_kernel(q_ref, k_ref, vt_ref, lamv_ref, g_ref, o_ref,
                      m_ref, l_ref, acc_ref, qq_ref, sa_ref, sb_ref, pa_ref, pb_ref,
                      ala_ref, alb_ref, *, lam_init):
    seq = q_ref.shape[2]
    tq, tk = ATT_Q, ATT_K
    lamv = lamv_ref[...]
    lam = (jnp.exp(jnp.sum(lamv[0:1] * lamv[1:2], axis=-1, keepdims=True))
           - jnp.exp(jnp.sum(lamv[2:3] * lamv[3:4], axis=-1, keepdims=True)) + lam_init)
    gain = g_ref[...] * (1.0 - lam_init)

    lane = lax.broadcasted_iota(jnp.int32, (tq, V_DIM), 1)
    key_chunk = lax.broadcasted_iota(jnp.int32, (tk, 2 * tq), 0) // CHUNK
    qry_chunk = (lax.broadcasted_iota(jnp.int32, (tk, 2 * tq), 1) % tq) // CHUNK
    diag_visible = key_chunk <= qry_chunk

    def scores(ki, s_ref):
        k0 = pl.multiple_of(ki * tk, tk)
        s_ref[...] = _nt_dot(k_ref[0, 0, pl.ds(k0, tk), :], qq_ref[...])

    def softmax(s_ref, p_ref, al_ref, masked):
        s = s_ref[...]
        if masked:
            s = jnp.where(diag_visible, s, NEG_BIG)
        m_old = m_ref[...]
        m_new = jnp.maximum(m_old, jnp.max(s, axis=0, keepdims=True))
        alpha = jnp.exp2(m_old - m_new)
        p = jnp.exp2(s - m_new)
        l_ref[...] = alpha * l_ref[...] + jnp.sum(p, axis=0, keepdims=True)
        m_ref[...] = m_new
        al_ref[...] = alpha
        p_ref[...] = p.astype(jnp.bfloat16)

    def values(ki, p_ref, al_ref):
        acc_ref[...] = al_ref[...] * acc_ref[...] + _nn_dot(vt_ref[0, ki], p_ref[...])

    def q_tile(qi, carry):
        q0 = pl.multiple_of(qi * tq, tq)
        qt = q_ref[0, 0, pl.ds(q0, tq), :]
        zero = jnp.zeros_like(qt)
        qq_ref[0:tq, :] = jnp.where(lane < HEAD_DIM, qt, zero)
        qq_ref[tq:2 * tq, :] = jnp.where(lane >= HEAD_DIM, qt, zero)
        m_ref[...] = jnp.full(m_ref.shape, NEG_BIG, jnp.float32)
        l_ref[...] = jnp.zeros(l_ref.shape, jnp.float32)
        acc_ref[...] = jnp.zeros(acc_ref.shape, jnp.float32)
        pb_ref[...] = jnp.zeros(pb_ref.shape, pb_ref.dtype)
        alb_ref[...] = jnp.ones(alb_ref.shape, jnp.float32)

        scores(0, sa_ref)

        def pair(i, c):
            scores(2 * i + 1, sb_ref)
            softmax(sa_ref, pa_ref, ala_ref, False)
            values(jnp.maximum(2 * i - 1, 0), pb_ref, alb_ref)
            scores(2 * i + 2, sa_ref)
            softmax(sb_ref, pb_ref, alb_ref, False)
            values(2 * i, pa_ref, ala_ref)
            return c

        lax.fori_loop(0, qi // 2, pair, 0)

        @pl.when(qi % 2 == 1)
        def _():
            scores(qi, sb_ref)
            softmax(sa_ref, pa_ref, ala_ref, False)
            values(jnp.maximum(qi - 2, 0), pb_ref, alb_ref)
            softmax(sb_ref, pb_ref, alb_ref, True)
            values(qi - 1, pa_ref, ala_ref)
            values(qi, pb_ref, alb_ref)

        @pl.when(qi % 2 == 0)
        def _():
            softmax(sa_ref, pa_ref, ala_ref, True)
            values(jnp.maximum(qi - 1, 0), pb_ref, alb_ref)
            values(qi, pa_ref, ala_ref)

        acc = acc_ref[...]
        inv_l = 1.0 / l_ref[...]
        o_t = acc[:, :tq] * inv_l[:, :tq] - lam * (acc[:, tq:] * inv_l[:, tq:])
        o = o_t.T
        ms = jnp.mean(o * o, axis=-1, keepdims=True)
        o_ref[0, 0, pl.ds(q0, tq), :] = (o * lax.rsqrt(ms + SUBLN_EPS) * gain).astype(o_ref.dtype)
        return carry

    lax.fori_loop(0, seq // tq, q_tile, 0)


def _diff_attn(q, k, vt, lamv, g_subln, lam_init):
    b, nh, s, dv = q.shape
    return pl.pallas_call(
        functools.partial(_diff_attn_kernel, lam_init=lam_init),
        grid=(b, nh),
        in_specs=[
            pl.BlockSpec((1, 1, s, dv), lambda bi, hi: (bi, hi, 0, 0)),
            pl.BlockSpec((1, 1, s, dv), lambda bi, hi: (bi, hi, 0, 0)),
            pl.BlockSpec((1, s // ATT_K, dv, ATT_K), lambda bi, hi: (bi, 0, hi, 0)),
            pl.BlockSpec((SUBLANES, LANES), lambda bi, hi: (0, 0)),
            pl.BlockSpec((1, dv), lambda bi, hi: (0, 0)),
        ],
        out_specs=pl.BlockSpec((1, 1, s, dv), lambda bi, hi: (bi, hi, 0, 0)),
        out_shape=jax.ShapeDtypeStruct((b, nh, s, dv), jnp.bfloat16),
        scratch_shapes=[
            pltpu.VMEM((1, 2 * ATT_Q), jnp.float32),
            pltpu.VMEM((1, 2 * ATT_Q), jnp.float32),
            pltpu.VMEM((dv, 2 * ATT_Q), jnp.float32),
            pltpu.VMEM((2 * ATT_Q, dv), jnp.bfloat16),
            pltpu.VMEM((ATT_K, 2 * ATT_Q), jnp.float32),
            pltpu.VMEM((ATT_K, 2 * ATT_Q), jnp.float32),
            pltpu.VMEM((ATT_K, 2 * ATT_Q), jnp.bfloat16),
            pltpu.VMEM((ATT_K, 2 * ATT_Q), jnp.bfloat16),
            pltpu.VMEM((1, 2 * ATT_Q), jnp.float32),
            pltpu.VMEM((1, 2 * ATT_Q), jnp.float32),
        ],
        compiler_params=pltpu.CompilerParams(
            dimension_semantics=("arbitrary", "arbitrary"), vmem_limit_bytes=VMEM_LIMIT),
        name="diff_attn",
    )(q, k, vt, lamv, g_subln)


def _rms(x, g):
    ms = jnp.mean(x * x, axis=-1, keepdims=True)
    return x * lax.rsqrt(ms + RMS_EPS) * g


def _ffn_out_kernel(x_ref, a_ref, o_ref, wout_ref, gffn_ref, wup_ref, wf_ref, bf_ref, wdn_ref,
                    gfin_ref, y_ref, gate_ref, *, final_norm):
    i = pl.program_id(1)
    rows = x_ref.shape[1]
    o_cat = jnp.concatenate([o_ref[0, hd] for hd in range(N_HEADS)], axis=-1)
    ao = jnp.concatenate([a_ref[0], o_cat], axis=-1)
    x1 = x_ref[0] + _nn_dot(ao, wout_ref[...])

    h = _rms(x1, gffn_ref[...]).astype(jnp.bfloat16)
    up = _nn_dot(h, wup_ref[...])
    val = up[:, FFN_DIM:]

    @pl.when(i == 0)
    def _():
        gate_ref[0:FFN_HALO, :] = jnp.zeros((FFN_HALO, FFN_DIM), jnp.float32)

    gate_ref[FFN_HALO:FFN_HALO + rows, :] = up[:, :FFN_DIM]
    conv = bf_ref[...]
    for tap in range(FFN_CONV_KERNEL):
        off = FFN_HALO - (FFN_CONV_KERNEL - 1) + tap
        conv = conv + wf_ref[tap:tap + 1, :] * gate_ref[off:off + rows, :]
    carry_rows = gate_ref[rows:rows + FFN_HALO, :]
    hidden = (conv * _sigmoid(conv) * val).astype(jnp.bfloat16)
    x2 = x1 + _nn_dot(hidden, wdn_ref[...])
    gate_ref[0:FFN_HALO, :] = carry_rows
    if final_norm:
        x2 = _rms(x2, gfin_ref[...])
    y_ref[0] = x2


def _ffn_out(x, a, o, w_out, g_ffn, w_up, w_fconv, b_fconv, w_down, g_final, final_norm):
    b, s, d = x.shape
    t = FFN_ROWS
    const = lambda bi, i: (0, 0)
    resident = functools.partial(pl.BlockSpec, index_map=const, pipeline_mode=pl.Buffered(1))
    return pl.pallas_call(
        functools.partial(_ffn_out_kernel, final_norm=final_norm),
        grid=(b, s // t),
        in_specs=[
            pl.BlockSpec((1, t, d), lambda bi, i: (bi, i, 0)),
            pl.BlockSpec((1, t, CONV_WIDTH), lambda bi, i: (bi, i, 0)),
            pl.BlockSpec((1, N_HEADS, t, V_DIM), lambda bi, i: (bi, 0, i, 0)),
            resident((d, d)),
            pl.BlockSpec((1, d), const),
            resident((d, 2 * FFN_DIM)),
            pl.BlockSpec((SUBLANES, FFN_DIM), const),
            pl.BlockSpec((1, FFN_DIM), const),
            resident((FFN_DIM, d)),
            pl.BlockSpec((1, d), const),
        ],
        out_specs=pl.BlockSpec((1, t, d), lambda bi, i: (bi, i, 0)),
        out_shape=jax.ShapeDtypeStruct((b, s, d), jnp.float32),
        scratch_shapes=[pltpu.VMEM((FFN_HALO + t, FFN_DIM), jnp.float32)],
        compiler_params=pltpu.CompilerParams(
            dimension_semantics=("arbitrary", "arbitrary"), vmem_limit_bytes=VMEM_LIMIT),
        name="ffn_out",
    )(x, a, o, w_out, g_ffn, w_up, w_fconv, b_fconv, w_down, g_final)


def _rope_tables(seq):
    inv_freq = 1.0 / (ROPE_THETA ** (jnp.arange(0, HEAD_DIM, 2, dtype=jnp.float32) / HEAD_DIM))
    ang = jnp.arange(seq, dtype=jnp.float32)[:, None] * inv_freq[None, :]
    cos, sin = jnp.cos(ang), jnp.sin(ang)
    reps = LANES // HEAD_DIM
    cos_t = jnp.tile(jnp.concatenate([cos, cos], axis=-1), (1, reps))
    sin_t = jnp.tile(jnp.concatenate([-sin, sin], axis=-1), (1, reps))
    return cos_t, sin_t


def _pad_rows(w, rows):
    return jnp.pad(w, ((0, rows - w.shape[0]), (0, 0)))


def kernel(x, g_mix, w_in, w_dw, b_dw, ln_g, ln_b, lambda_q1, lambda_k1, lambda_q2, lambda_k2,
           g_subln, w_out, g_ffn, w_up, w_fconv, b_fconv, w_down, g_final):
    depth = w_in.shape[0]
    seq = x.shape[1]
    cos_t, sin_t = _rope_tables(seq)
    bf = jnp.bfloat16
    row = lambda v: v.reshape(1, -1).astype(jnp.float32)
    for layer in range(depth):
        w_main = w_in[layer][:, :QK_COLS].astype(bf)
        w_vt = w_in[layer][:, QK_COLS:].T.astype(bf)
        a_pre, q, k, vt = _mix_in(x, row(g_mix[layer]), w_main, w_vt, cos_t, sin_t)
        a = _conv_mod(a_pre, _pad_rows(w_dw[layer], CONV_HALO), row(b_dw[layer]),
                      row(ln_g[layer]), row(ln_b[layer]))
        lam_init = 0.8 - 0.6 * math.exp(-0.3 * layer)
        lam_rows = jnp.stack([lambda_q1[layer], lambda_k1[layer], lambda_q2[layer], lambda_k2[layer]])
        lamv = jnp.pad(lam_rows.astype(jnp.float32),
                       ((0, SUBLANES - 4), (0, LANES - lam_rows.shape[1])))
        o = _diff_attn(q, k, vt, lamv, row(g_subln[layer]), lam_init)
        x = _ffn_out(x, a, o, w_out[layer].astype(bf), row(g_ffn[layer]), w_up[layer].astype(bf),
                     _pad_rows(w_fconv[layer], SUBLANES), row(b_fconv[layer]),
                     w_down[layer].astype(bf), row(g_final), final_norm=(layer == depth - 1))
    return x
```

```python
import functools
import math

import jax
import jax.numpy as jnp
from jax import lax
from jax.experimental import pallas as pl
from jax.experimental.pallas import tpu as pltpu

D_MODEL = 1024
CHUNK = 64
CONV_WIDTH = 512
ATTN_WIDTH = 512
N_HEADS = 4
HEAD_DIM = 64
V_DIM = 2 * HEAD_DIM
CONV_KERNEL = 31
FFN_DIM = 2816
FFN_CONV_KERNEL = 3
ROPE_THETA = 10000.0
RMS_EPS = 1e-6
LN_EPS = 1e-5
SUBLN_EPS = 1e-5
QK_COLS = 2 * CONV_WIDTH + 2 * ATTN_WIDTH

LANES = 128
SUBLANES = 8
LOG2E = math.log2(math.e)
NEG_BIG = -1e30

MIX_ROWS = 512
CONV_ROWS = 512
CONV_HALO = 32
CONV_CHUNK = 32
ATT_Q = 512
ATT_K = 512
FFN_ROWS = 256
FFN_HALO = SUBLANES
FFN_BLOCK = 256

VMEM_LIMIT = 56 * 1024 * 1024


def _nt_dot(a, b):
    return lax.dot_general(a, b, (((1,), (1,)), ((), ())), preferred_element_type=jnp.float32)


def _nn_dot(a, b):
    return jnp.dot(a, b, preferred_element_type=jnp.float32)


def _sigmoid(x):
    return 1.0 / (1.0 + jnp.exp(-x))


def _mix_in_kernel(x_ref, g_ref, w_ref, wvt_ref, cos_ref, sin_ref,
                   apre_ref, q_ref, k_ref, vt_ref, *, q_scale):
    x = x_ref[0]
    ms = jnp.mean(x * x, axis=-1, keepdims=True)
    h = (x * lax.rsqrt(ms + RMS_EPS) * g_ref[...]).astype(jnp.bfloat16)

    main = _nn_dot(h, w_ref[...])
    c_val = main[:, :CONV_WIDTH]
    c_gate = main[:, CONV_WIDTH:2 * CONV_WIDTH]
    apre_ref[0] = c_val * _sigmoid(c_gate)

    cos = cos_ref[...]
    sin = sin_ref[...]
    lane = lax.broadcasted_iota(jnp.int32, cos.shape, 1)
    first_half = (lane % HEAD_DIM) < (HEAD_DIM // 2)

    def rope(t):
        up = pltpu.roll(t, LANES - HEAD_DIM // 2, axis=1)
        down = pltpu.roll(t, HEAD_DIM // 2, axis=1)
        return t * cos + jnp.where(first_half, up, down) * sin

    q0 = 2 * CONV_WIDTH
    k0 = q0 + ATTN_WIDTH
    for hd in range(N_HEADS):
        qh = main[:, q0 + hd * V_DIM:q0 + (hd + 1) * V_DIM]
        kh = main[:, k0 + hd * V_DIM:k0 + (hd + 1) * V_DIM]
        q_ref[0, hd] = (rope(qh) * q_scale).astype(jnp.bfloat16)
        k_ref[0, hd] = rope(kh).astype(jnp.bfloat16)

    vt = _nt_dot(wvt_ref[...], h).astype(jnp.bfloat16)
    rows = x.shape[0]
    for j in range(rows // ATT_K):
        vt_ref[0, j] = vt[:, j * ATT_K:(j + 1) * ATT_K]


def _mix_in(x, g_mix, w_main, w_vt, cos_t, sin_t):
    b, s, d = x.shape
    t = MIX_ROWS
    q_scale = HEAD_DIM ** -0.5 * LOG2E
    const = lambda bi, i: (0, 0)
    return pl.pallas_call(
        functools.partial(_mix_in_kernel, q_scale=q_scale),
        grid=(b, s // t),
        in_specs=[
            pl.BlockSpec((1, t, d), lambda bi, i: (bi, i, 0)),
            pl.BlockSpec((1, d), const),
            pl.BlockSpec((d, QK_COLS), const),
            pl.BlockSpec((ATTN_WIDTH, d), const),
            pl.BlockSpec((t, LANES), lambda bi, i: (i, 0)),
            pl.BlockSpec((t, LANES), lambda bi, i: (i, 0)),
        ],
        out_specs=[
            pl.BlockSpec((1, t, CONV_WIDTH), lambda bi, i: (bi, i, 0)),
            pl.BlockSpec((1, N_HEADS, t, V_DIM), lambda bi, i: (bi, 0, i, 0)),
            pl.BlockSpec((1, N_HEADS, t, V_DIM), lambda bi, i: (bi, 0, i, 0)),
            pl.BlockSpec((1, t // ATT_K, ATTN_WIDTH, ATT_K), lambda bi, i: (bi, i, 0, 0)),
        ],
        out_shape=[
            jax.ShapeDtypeStruct((b, s, CONV_WIDTH), jnp.float32),
            jax.ShapeDtypeStruct((b, N_HEADS, s, V_DIM), jnp.bfloat16),
            jax.ShapeDtypeStruct((b, N_HEADS, s, V_DIM), jnp.bfloat16),
            jax.ShapeDtypeStruct((b, s // ATT_K, ATTN_WIDTH, ATT_K), jnp.bfloat16),
        ],
        compiler_params=pltpu.CompilerParams(
            dimension_semantics=("arbitrary", "arbitrary"), vmem_limit_bytes=VMEM_LIMIT),
        name="mix_in",
    )(x, g_mix, w_main, w_vt, cos_t, sin_t)


def _conv_mod_kernel(cur_ref, halo_ref, w_ref, b_ref, lng_ref, lnb_ref, a_ref, u_ref):
    i = pl.program_id(1)
    rows = cur_ref.shape[1]
    halo = halo_ref[0]
    u_ref[0, 0:CONV_HALO, :] = jnp.where(i == 0, jnp.zeros_like(halo), halo)
    u_ref[0, CONV_HALO:CONV_HALO + rows, :] = cur_ref[0]
    span = CONV_HALO + rows - SUBLANES
    for r in range(1, SUBLANES):
        u_ref[r, 0:span, :] = u_ref[0, r:r + span, :]

    bias = b_ref[...]
    lng = lng_ref[...]
    lnb = lnb_ref[...]
    first_tap = CONV_HALO - (CONV_KERNEL - 1)

    groups = CONV_CHUNK // SUBLANES

    def chunk(c, carry):
        base = c * CONV_CHUNK
        accs = [jnp.broadcast_to(bias, (CONV_CHUNK, CONV_WIDTH)), None]
        for tap in range(CONV_KERNEL):
            off = first_tap + tap
            aligned = base + (off // SUBLANES) * SUBLANES
            w_tap = jnp.concatenate([w_ref[tap]] * groups, axis=0)
            term = w_tap * u_ref[off % SUBLANES, pl.ds(aligned, CONV_CHUNK), :]
            accs[tap % 2] = term if accs[tap % 2] is None else accs[tap % 2] + term
        acc = accs[0] + accs[1]
        mu = jnp.mean(acc, axis=-1, keepdims=True)
        cen = acc - mu
        var = jnp.mean(cen * cen, axis=-1, keepdims=True)
        y = cen * lax.rsqrt(var + LN_EPS) * lng + lnb
        a_ref[0, pl.ds(base, CONV_CHUNK), :] = (y * _sigmoid(y)).astype(a_ref.dtype)
        return carry

    for c in range(rows // CONV_CHUNK):
        chunk(c, 0)


def _conv_mod(a_pre, w_dw, b_dw, ln_g, ln_b):
    b, s, c = a_pre.shape
    t = CONV_ROWS
    per = t // CONV_HALO
    const = lambda bi, i: (0, 0)
    return pl.pallas_call(
        _conv_mod_kernel,
        grid=(b, s // t),
        in_specs=[
            pl.BlockSpec((1, t, c), lambda bi, i: (bi, i, 0)),
            pl.BlockSpec((1, CONV_HALO, c), lambda bi, i: (bi, jnp.maximum(i * per - 1, 0), 0)),
            pl.BlockSpec((CONV_KERNEL, SUBLANES, c), lambda bi, i: (0, 0, 0)),
            pl.BlockSpec((1, c), const),
            pl.BlockSpec((1, c), const),
            pl.BlockSpec((1, c), const),
        ],
        out_specs=pl.BlockSpec((1, t, c), lambda bi, i: (bi, i, 0)),
        out_shape=jax.ShapeDtypeStruct((b, s, c), jnp.bfloat16),
        scratch_shapes=[pltpu.VMEM((SUBLANES, CONV_HALO + t, c), jnp.float32)],
        compiler_params=pltpu.CompilerParams(
            dimension_semantics=("arbitrary", "arbitrary"), vmem_limit_bytes=VMEM_LIMIT),
        name="conv_mod",
    )(a_pre, a_pre, w_dw, b_dw, ln_g, ln_b)


def _diff_attn_kernel(q_ref, k_ref, vt_ref, lamv_ref, g_ref, o_ref,
                      m_ref, l_ref, acc_ref, qq_ref, sa_ref, sb_ref, pa_ref, pb_ref,
                      ala_ref, alb_ref, mask_ref, *, lam_init):
    seq = q_ref.shape[2]
    tq, tk = ATT_Q, ATT_K
    lamv = lamv_ref[...]
    lam = (jnp.exp(jnp.sum(lamv[0:1] * lamv[1:2], axis=-1, keepdims=True))
           - jnp.exp(jnp.sum(lamv[2:3] * lamv[3:4], axis=-1, keepdims=True)) + lam_init)
    gain = g_ref[...] * (1.0 - lam_init)

    lane = lax.broadcasted_iota(jnp.int32, (tq, V_DIM), 1)
    key_chunk = lax.broadcasted_iota(jnp.int32, (tk, 2 * tq), 0) // CHUNK
    qry_chunk = (lax.broadcasted_iota(jnp.int32, (tk, 2 * tq), 1) % tq) // CHUNK
    mask_ref[...] = jnp.where(key_chunk <= qry_chunk, 0.0, NEG_BIG)

    def scores(ki, s_ref):
        k0 = pl.multiple_of(ki * tk, tk)
        s_ref[...] = _nt_dot(k_ref[0, 0, pl.ds(k0, tk), :], qq_ref[...])

    def softmax(s_ref, p_ref, al_ref, masked):
        s = s_ref[...]
        if masked:
            s = s + mask_ref[...]
        m_old = m_ref[...]
        m_new = jnp.maximum(m_old, jnp.max(s, axis=0, keepdims=True))
        alpha = jnp.exp2(m_old - m_new)
        p = jnp.exp2(s - m_new)
        l_ref[...] = alpha * l_ref[...] + jnp.sum(p, axis=0, keepdims=True)
        m_ref[...] = m_new
        al_ref[...] = alpha
        p_ref[...] = p.astype(jnp.bfloat16)

    def values(ki, p_ref, al_ref):
        acc_ref[...] = al_ref[...] * acc_ref[...] + _nn_dot(vt_ref[0, ki], p_ref[...])

    def q_tile(qi, carry):
        q0 = pl.multiple_of(qi * tq, tq)
        qt = q_ref[0, 0, pl.ds(q0, tq), :]
        zero = jnp.zeros_like(qt)
        qq_ref[0:tq, :] = jnp.where(lane < HEAD_DIM, qt, zero)
        qq_ref[tq:2 * tq, :] = jnp.where(lane >= HEAD_DIM, qt, zero)
        m_ref[...] = jnp.full(m_ref.shape, NEG_BIG, jnp.float32)
        l_ref[...] = jnp.zeros(l_ref.shape, jnp.float32)
        acc_ref[...] = jnp.zeros(acc_ref.shape, jnp.float32)
        pb_ref[...] = jnp.zeros(pb_ref.shape, pb_ref.dtype)
        alb_ref[...] = jnp.ones(alb_ref.shape, jnp.float32)

        scores(0, sa_ref)

        def pair(i, c):
            scores(2 * i + 1, sb_ref)
            softmax(sa_ref, pa_ref, ala_ref, False)
            values(jnp.maximum(2 * i - 1, 0), pb_ref, alb_ref)
            scores(2 * i + 2, sa_ref)
            softmax(sb_ref, pb_ref, alb_ref, False)
            values(2 * i, pa_ref, ala_ref)
            return c

        lax.fori_loop(0, qi // 2, pair, 0)

        @pl.when(qi % 2 == 1)
        def _():
            scores(qi, sb_ref)
            softmax(sa_ref, pa_ref, ala_ref, False)
            values(jnp.maximum(qi - 2, 0), pb_ref, alb_ref)
            softmax(sb_ref, pb_ref, alb_ref, True)
            values(qi - 1, pa_ref, ala_ref)
            values(qi, pb_ref, alb_ref)

        @pl.when(qi % 2 == 0)
        def _():
            softmax(sa_ref, pa_ref, ala_ref, True)
            values(jnp.maximum(qi - 1, 0), pb_ref, alb_ref)
            values(qi, pa_ref, ala_ref)

        acc = acc_ref[...]
        inv_l = 1.0 / l_ref[...]
        o_t = acc[:, :tq] * inv_l[:, :tq] - lam * (acc[:, tq:] * inv_l[:, tq:])
        o = o_t.T
        ms = jnp.mean(o * o, axis=-1, keepdims=True)
        o_ref[0, 0, pl.ds(q0, tq), :] = (o * lax.rsqrt(ms + SUBLN_EPS) * gain).astype(o_ref.dtype)
        return carry

    lax.fori_loop(0, seq // tq, q_tile, 0)


def _diff_attn(q, k, vt, lamv, g_subln, lam_init):
    b, nh, s, dv = q.shape
    return pl.pallas_call(
        functools.partial(_diff_attn_kernel, lam_init=lam_init),
        grid=(b, nh),
        in_specs=[
            pl.BlockSpec((1, 1, s, dv), lambda bi, hi: (bi, hi, 0, 0)),
            pl.BlockSpec((1, 1, s, dv), lambda bi, hi: (bi, hi, 0, 0)),
            pl.BlockSpec((1, s // ATT_K, dv, ATT_K), lambda bi, hi: (bi, 0, hi, 0)),
            pl.BlockSpec((SUBLANES, LANES), lambda bi, hi: (0, 0)),
            pl.BlockSpec((1, dv), lambda bi, hi: (0, 0)),
        ],
        out_specs=pl.BlockSpec((1, 1, s, dv), lambda bi, hi: (bi, hi, 0, 0)),
        out_shape=jax.ShapeDtypeStruct((b, nh, s, dv), jnp.bfloat16),
        scratch_shapes=[
            pltpu.VMEM((1, 2 * ATT_Q), jnp.float32),
            pltpu.VMEM((1, 2 * ATT_Q), jnp.float32),
            pltpu.VMEM((dv, 2 * ATT_Q), jnp.float32),
            pltpu.VMEM((2 * ATT_Q, dv), jnp.bfloat16),
            pltpu.VMEM((ATT_K, 2 * ATT_Q), jnp.float32),
            pltpu.VMEM((ATT_K, 2 * ATT_Q), jnp.float32),
            pltpu.VMEM((ATT_K, 2 * ATT_Q), jnp.bfloat16),
            pltpu.VMEM((ATT_K, 2 * ATT_Q), jnp.bfloat16),
            pltpu.VMEM((1, 2 * ATT_Q), jnp.float32),
            pltpu.VMEM((1, 2 * ATT_Q), jnp.float32),
            pltpu.VMEM((ATT_K, 2 * ATT_Q), jnp.float32),
        ],
        compiler_params=pltpu.CompilerParams(
            dimension_semantics=("arbitrary", "arbitrary"), vmem_limit_bytes=VMEM_LIMIT),
        name="diff_attn",
    )(q, k, vt, lamv, g_subln)


def _rms(x, g):
    ms = jnp.mean(x * x, axis=-1, keepdims=True)
    return x * lax.rsqrt(ms + RMS_EPS) * g


def _ffn_out_kernel(x_ref, a_ref, o_ref, wout_ref, gffn_ref, wup_ref, wf_ref, bf_ref, wdn_ref,
                    gfin_ref, y_ref, gate_ref, *, final_norm):
    i = pl.program_id(1)
    rows = x_ref.shape[1]
    o_cat = jnp.concatenate([o_ref[0, hd] for hd in range(N_HEADS)], axis=-1)
    ao = jnp.concatenate([a_ref[0], o_cat], axis=-1)
    x1 = x_ref[0] + _nn_dot(ao, wout_ref[...])

    h = _rms(x1, gffn_ref[...]).astype(jnp.bfloat16)

    @pl.when(i == 0)
    def _():
        gate_ref[0:FFN_HALO, :] = jnp.zeros((FFN_HALO, FFN_DIM), jnp.float32)

    def up_proj(j):
        return _nn_dot(h, wup_ref[:, 2 * j * FFN_BLOCK:2 * (j + 1) * FFN_BLOCK])

    n_blocks = FFN_DIM // FFN_BLOCK
    x2 = x1
    up_next = up_proj(0)
    for j in range(n_blocks):
        c0 = j * FFN_BLOCK
        up = up_next
        if j + 1 < n_blocks:
            up_next = up_proj(j + 1)
        gate_ref[FFN_HALO:FFN_HALO + rows, c0:c0 + FFN_BLOCK] = up[:, :FFN_BLOCK]
        conv = bf_ref[:, c0:c0 + FFN_BLOCK]
        for tap in range(FFN_CONV_KERNEL):
            off = FFN_HALO - (FFN_CONV_KERNEL - 1) + tap
            conv = conv + wf_ref[tap:tap + 1, c0:c0 + FFN_BLOCK] * gate_ref[off:off + rows, c0:c0 + FFN_BLOCK]
        carry_rows = gate_ref[rows:rows + FFN_HALO, c0:c0 + FFN_BLOCK]
        hidden = (conv * _sigmoid(conv) * up[:, FFN_BLOCK:]).astype(jnp.bfloat16)
        x2 = x2 + _nn_dot(hidden, wdn_ref[c0:c0 + FFN_BLOCK, :])
        gate_ref[0:FFN_HALO, c0:c0 + FFN_BLOCK] = carry_rows
    if final_norm:
        x2 = _rms(x2, gfin_ref[...])
    y_ref[0] = x2


def _ffn_out(x, a, o, w_out, g_ffn, w_up, w_fconv, b_fconv, w_down, g_final, final_norm):
    b, s, d = x.shape
    t = FFN_ROWS
    const = lambda bi, i: (0, 0)
    resident = functools.partial(pl.BlockSpec, index_map=const, pipeline_mode=pl.Buffered(1))
    return pl.pallas_call(
        functools.partial(_ffn_out_kernel, final_norm=final_norm),
        grid=(b, s // t),
        in_specs=[
            pl.BlockSpec((1, t, d), lambda bi, i: (bi, i, 0)),
            pl.BlockSpec((1, t, CONV_WIDTH), lambda bi, i: (bi, i, 0)),
            pl.BlockSpec((1, N_HEADS, t, V_DIM), lambda bi, i: (bi, 0, i, 0)),
            resident((d, d)),
            pl.BlockSpec((1, d), const),
            resident((d, 2 * FFN_DIM)),
            pl.BlockSpec((SUBLANES, FFN_DIM), const),
            pl.BlockSpec((1, FFN_DIM), const),
            resident((FFN_DIM, d)),
            pl.BlockSpec((1, d), const),
        ],
        out_specs=pl.BlockSpec((1, t, d), lambda bi, i: (bi, i, 0)),
        out_shape=jax.ShapeDtypeStruct((b, s, d), jnp.float32),
        scratch_shapes=[pltpu.VMEM((FFN_HALO + t, FFN_DIM), jnp.float32)],
        compiler_params=pltpu.CompilerParams(
            dimension_semantics=("arbitrary", "arbitrary"), vmem_limit_bytes=VMEM_LIMIT),
        name="ffn_out",
    )(x, a, o, w_out, g_ffn, w_up, w_fconv, b_fconv, w_down, g_final)


def _rope_tables(seq):
    inv_freq = 1.0 / (ROPE_THETA ** (jnp.arange(0, HEAD_DIM, 2, dtype=jnp.float32) / HEAD_DIM))
    ang = jnp.arange(seq, dtype=jnp.float32)[:, None] * inv_freq[None, :]
    cos, sin = jnp.cos(ang), jnp.sin(ang)
    reps = LANES // HEAD_DIM
    cos_t = jnp.tile(jnp.concatenate([cos, cos], axis=-1), (1, reps))
    sin_t = jnp.tile(jnp.concatenate([-sin, sin], axis=-1), (1, reps))
    return cos_t, sin_t


def _pad_rows(w, rows):
    return jnp.pad(w, ((0, rows - w.shape[0]), (0, 0)))


def kernel(x, g_mix, w_in, w_dw, b_dw, ln_g, ln_b, lambda_q1, lambda_k1, lambda_q2, lambda_k2,
           g_subln, w_out, g_ffn, w_up, w_fconv, b_fconv, w_down, g_final):
    depth = w_in.shape[0]
    seq = x.shape[1]
    cos_t, sin_t = _rope_tables(seq)
    bf = jnp.bfloat16
    row = lambda v: v.reshape(1, -1).astype(jnp.float32)
    for layer in range(depth):
        w_main = w_in[layer][:, :QK_COLS].astype(bf)
        w_vt = w_in[layer][:, QK_COLS:].T.astype(bf)
        a_pre, q, k, vt = _mix_in(x, row(g_mix[layer]), w_main, w_vt, cos_t, sin_t)
        w_taps = jnp.broadcast_to(w_dw[layer].astype(jnp.float32)[:, None, :],
                                  (CONV_KERNEL, SUBLANES, CONV_WIDTH))
        a = _conv_mod(a_pre, w_taps, row(b_dw[layer]), row(ln_g[layer]), row(ln_b[layer]))
        lam_init = 0.8 - 0.6 * math.exp(-0.3 * layer)
        lam_rows = jnp.stack([lambda_q1[layer], lambda_k1[layer], lambda_q2[layer], lambda_k2[layer]])
        lamv = jnp.pad(lam_rows.astype(jnp.float32),
                       ((0, SUBLANES - 4), (0, LANES - lam_rows.shape[1])))
        o = _diff_attn(q, k, vt, lamv, row(g_subln[layer]), lam_init)
        w_up_blocks = (w_up[layer].astype(bf)
                       .reshape(D_MODEL, 2, FFN_DIM // FFN_BLOCK, FFN_BLOCK)
                       .transpose(0, 2, 1, 3).reshape(D_MODEL, 2 * FFN_DIM))
        x = _ffn_out(x, a, o, w_out[layer].astype(bf), row(g_ffn[layer]), w_up_blocks,
                     _pad_rows(w_fconv[layer], SUBLANES), row(b_fconv[layer]),
                     w_down[layer].astype(bf), row(g_final), final_norm=(layer == depth - 1))
    return x
```

```python
import functools
import math

import jax
import jax.numpy as jnp
from jax import lax
from jax.experimental import pallas as pl
from jax.experimental.pallas import tpu as pltpu

D_MODEL = 1024
CHUNK = 64
CONV_WIDTH = 512
ATTN_WIDTH = 512
N_HEADS = 4
HEAD_DIM = 64
V_DIM = 2 * HEAD_DIM
CONV_KERNEL = 31
FFN_DIM = 2816
FFN_CONV_KERNEL = 3
ROPE_THETA = 10000.0
RMS_EPS = 1e-6
LN_EPS = 1e-5
SUBLN_EPS = 1e-5
QK_COLS = 2 * CONV_WIDTH + 2 * ATTN_WIDTH

LANES = 128
SUBLANES = 8
LOG2E = math.log2(math.e)
NEG_BIG = -1e30

MIX_ROWS = 512
CONV_ROWS = 512
CONV_HALO = 32
CONV_CHUNK = 32
ATT_Q = 512
ATT_K = 512
FFN_ROWS = 256
FFN_HALO = SUBLANES
FFN_BLOCK = 256

VMEM_LIMIT = 56 * 1024 * 1024


def _nt_dot(a, b):
    return lax.dot_general(a, b, (((1,), (1,)), ((), ())), preferred_element_type=jnp.float32)


def _nn_dot(a, b):
    return jnp.dot(a, b, preferred_element_type=jnp.float32)


def _sigmoid(x):
    return 1.0 / (1.0 + jnp.exp(-x))


def _mix_in_kernel(x_ref, g_ref, w_ref, wvt_ref, cos_ref, sin_ref,
                   apre_ref, q_ref, k_ref, vt_ref, *, q_scale):
    x = x_ref[0]
    ms = jnp.mean(x * x, axis=-1, keepdims=True)
    h = (x * lax.rsqrt(ms + RMS_EPS) * g_ref[...]).astype(jnp.bfloat16)

    main = _nn_dot(h, w_ref[...])
    c_val = main[:, :CONV_WIDTH]
    c_gate = main[:, CONV_WIDTH:2 * CONV_WIDTH]
    apre_ref[0] = c_val * _sigmoid(c_gate)

    cos = cos_ref[...]
    sin = sin_ref[...]
    lane = lax.broadcasted_iota(jnp.int32, cos.shape, 1)
    first_half = (lane % HEAD_DIM) < (HEAD_DIM // 2)

    def rope(t):
        up = pltpu.roll(t, LANES - HEAD_DIM // 2, axis=1)
        down = pltpu.roll(t, HEAD_DIM // 2, axis=1)
        return t * cos + jnp.where(first_half, up, down) * sin

    q0 = 2 * CONV_WIDTH
    k0 = q0 + ATTN_WIDTH
    for hd in range(N_HEADS):
        qh = main[:, q0 + hd * V_DIM:q0 + (hd + 1) * V_DIM]
        kh = main[:, k0 + hd * V_DIM:k0 + (hd + 1) * V_DIM]
        q_ref[0, hd] = (rope(qh) * q_scale).astype(jnp.bfloat16)
        k_ref[0, hd] = rope(kh).astype(jnp.bfloat16)

    vt = _nt_dot(wvt_ref[...], h).astype(jnp.bfloat16)
    rows = x.shape[0]
    for j in range(rows // ATT_K):
        vt_ref[0, j] = vt[:, j * ATT_K:(j + 1) * ATT_K]


def _mix_in(x, g_mix, w_main, w_vt, cos_t, sin_t):
    b, s, d = x.shape
    t = MIX_ROWS
    q_scale = HEAD_DIM ** -0.5 * LOG2E
    const = lambda bi, i: (0, 0)
    return pl.pallas_call(
        functools.partial(_mix_in_kernel, q_scale=q_scale),
        grid=(b, s // t),
        in_specs=[
            pl.BlockSpec((1, t, d), lambda bi, i: (bi, i, 0)),
            pl.BlockSpec((1, d), const),
            pl.BlockSpec((d, QK_COLS), const),
            pl.BlockSpec((ATTN_WIDTH, d), const),
            pl.BlockSpec((t, LANES), lambda bi, i: (i, 0)),
            pl.BlockSpec((t, LANES), lambda bi, i: (i, 0)),
        ],
        out_specs=[
            pl.BlockSpec((1, t, CONV_WIDTH), lambda bi, i: (bi, i, 0)),
            pl.BlockSpec((1, N_HEADS, t, V_DIM), lambda bi, i: (bi, 0, i, 0)),
            pl.BlockSpec((1, N_HEADS, t, V_DIM), lambda bi, i: (bi, 0, i, 0)),
            pl.BlockSpec((1, t // ATT_K, ATTN_WIDTH, ATT_K), lambda bi, i: (bi, i, 0, 0)),
        ],
        out_shape=[
            jax.ShapeDtypeStruct((b, s, CONV_WIDTH), jnp.float32),
            jax.ShapeDtypeStruct((b, N_HEADS, s, V_DIM), jnp.bfloat16),
            jax.ShapeDtypeStruct((b, N_HEADS, s, V_DIM), jnp.bfloat16),
            jax.ShapeDtypeStruct((b, s // ATT_K, ATTN_WIDTH, ATT_K), jnp.bfloat16),
        ],
        compiler_params=pltpu.CompilerParams(
            dimension_semantics=("arbitrary", "arbitrary"), vmem_limit_bytes=VMEM_LIMIT),
        name="mix_in",
    )(x, g_mix, w_main, w_vt, cos_t, sin_t)


def _conv_mod_kernel(cur_ref, halo_ref, w_ref, b_ref, lng_ref, lnb_ref, a_ref, u_ref):
    i = pl.program_id(1)
    rows = cur_ref.shape[1]
    halo = halo_ref[0]
    u_ref[0, 0:CONV_HALO, :] = jnp.where(i == 0, jnp.zeros_like(halo), halo)
    u_ref[0, CONV_HALO:CONV_HALO + rows, :] = cur_ref[0]
    span = CONV_HALO + rows - SUBLANES
    for r in range(1, SUBLANES):
        u_ref[r, 0:span, :] = u_ref[0, r:r + span, :]

    bias = b_ref[...]
    lng = lng_ref[...]
    lnb = lnb_ref[...]
    first_tap = CONV_HALO - (CONV_KERNEL - 1)

    groups = CONV_CHUNK // SUBLANES

    def chunk(c, carry):
        base = c * CONV_CHUNK
        accs = [jnp.broadcast_to(bias, (CONV_CHUNK, CONV_WIDTH)), None]
        for tap in range(CONV_KERNEL):
            off = first_tap + tap
            aligned = base + (off // SUBLANES) * SUBLANES
            w_tap = jnp.concatenate([w_ref[tap]] * groups, axis=0)
            term = w_tap * u_ref[off % SUBLANES, pl.ds(aligned, CONV_CHUNK), :]
            accs[tap % 2] = term if accs[tap % 2] is None else accs[tap % 2] + term
        acc = accs[0] + accs[1]
        mu = jnp.mean(acc, axis=-1, keepdims=True)
        cen = acc - mu
        var = jnp.mean(cen * cen, axis=-1, keepdims=True)
        y = cen * lax.rsqrt(var + LN_EPS) * lng + lnb
        a_ref[0, pl.ds(base, CONV_CHUNK), :] = (y * _sigmoid(y)).astype(a_ref.dtype)
        return carry

    for c in range(rows // CONV_CHUNK):
        chunk(c, 0)


def _conv_mod(a_pre, w_dw, b_dw, ln_g, ln_b):
    b, s, c = a_pre.shape
    t = CONV_ROWS
    per = t // CONV_HALO
    const = lambda bi, i: (0, 0)
    return pl.pallas_call(
        _conv_mod_kernel,
        grid=(b, s // t),
        in_specs=[
            pl.BlockSpec((1, t, c), lambda bi, i: (bi, i, 0)),
            pl.BlockSpec((1, CONV_HALO, c), lambda bi, i: (bi, jnp.maximum(i * per - 1, 0), 0)),
            pl.BlockSpec((CONV_KERNEL, SUBLANES, c), lambda bi, i: (0, 0, 0)),
            pl.BlockSpec((1, c), const),
            pl.BlockSpec((1, c), const),
            pl.BlockSpec((1, c), const),
        ],
        out_specs=pl.BlockSpec((1, t, c), lambda bi, i: (bi, i, 0)),
        out_shape=jax.ShapeDtypeStruct((b, s, c), jnp.bfloat16),
        scratch_shapes=[pltpu.VMEM((SUBLANES, CONV_HALO + t, c), jnp.float32)],
        compiler_params=pltpu.CompilerParams(
            dimension_semantics=("arbitrary", "arbitrary"), vmem_limit_bytes=VMEM_LIMIT),
        name="conv_mod",
    )(a_pre, a_pre, w_dw, b_dw, ln_g, ln_b)


def _diff_attn_kernel(q_ref, k_ref, vt_ref, lamv_ref, g_ref, o_ref,
                      m_ref, l_ref, acc_ref, qq_ref, sa_ref, sb_ref, mask_ref, *, lam_init):
    seq = q_ref.shape[2]
    tq, tk = ATT_Q, ATT_K
    lamv = lamv_ref[...]
    lam = (jnp.exp(jnp.sum(lamv[0:1] * lamv[1:2], axis=-1, keepdims=True))
           - jnp.exp(jnp.sum(lamv[2:3] * lamv[3:4], axis=-1, keepdims=True)) + lam_init)
    gain = g_ref[...] * (1.0 - lam_init)

    lane = lax.broadcasted_iota(jnp.int32, (tq, V_DIM), 1)
    key_chunk = lax.broadcasted_iota(jnp.int32, (tk, 2 * tq), 0) // CHUNK
    qry_chunk = (lax.broadcasted_iota(jnp.int32, (tk, 2 * tq), 1) % tq) // CHUNK
    mask_ref[...] = jnp.where(key_chunk <= qry_chunk, 0.0, NEG_BIG)

    def scores(ki, s_ref):
        k0 = pl.multiple_of(ki * tk, tk)
        s_ref[...] = _nt_dot(k_ref[0, 0, pl.ds(k0, tk), :], qq_ref[...])

    def absorb(ki, s_ref, masked):
        s = s_ref[...]
        if masked:
            s = s + mask_ref[...]
        m_old = m_ref[...]
        m_new = jnp.maximum(m_old, jnp.max(s, axis=0, keepdims=True))
        alpha = jnp.exp2(m_old - m_new)
        p = jnp.exp2(s - m_new)
        l_ref[...] = alpha * l_ref[...] + jnp.sum(p, axis=0, keepdims=True)
        m_ref[...] = m_new
        acc_ref[...] = alpha * acc_ref[...] + _nn_dot(vt_ref[0, ki], p.astype(jnp.bfloat16))

    def q_tile(qi, carry):
        q0 = pl.multiple_of(qi * tq, tq)
        qt = q_ref[0, 0, pl.ds(q0, tq), :]
        zero = jnp.zeros_like(qt)
        qq_ref[0:tq, :] = jnp.where(lane < HEAD_DIM, qt, zero)
        qq_ref[tq:2 * tq, :] = jnp.where(lane >= HEAD_DIM, qt, zero)
        m_ref[...] = jnp.full(m_ref.shape, NEG_BIG, jnp.float32)
        l_ref[...] = jnp.zeros(l_ref.shape, jnp.float32)
        acc_ref[...] = jnp.zeros(acc_ref.shape, jnp.float32)

        scores(0, sa_ref)

        def pair(i, c):
            scores(2 * i + 1, sb_ref)
            absorb(2 * i, sa_ref, False)
            scores(2 * i + 2, sa_ref)
            absorb(2 * i + 1, sb_ref, False)
            return c

        lax.fori_loop(0, qi // 2, pair, 0)

        @pl.when(qi % 2 == 1)
        def _():
            scores(qi, sb_ref)
            absorb(qi - 1, sa_ref, False)
            absorb(qi, sb_ref, True)

        @pl.when(qi % 2 == 0)
        def _():
            absorb(qi, sa_ref, True)

        acc = acc_ref[...]
        inv_l = 1.0 / l_ref[...]
        o_t = acc[:, :tq] * inv_l[:, :tq] - lam * (acc[:, tq:] * inv_l[:, tq:])
        o = o_t.T
        ms = jnp.mean(o * o, axis=-1, keepdims=True)
        o_ref[0, 0, pl.ds(q0, tq), :] = (o * lax.rsqrt(ms + SUBLN_EPS) * gain).astype(o_ref.dtype)
        return carry

    lax.fori_loop(0, seq // tq, q_tile, 0)


def _diff_attn(q, k, vt, lamv, g_subln, lam_init):
    b, nh, s, dv = q.shape
    return pl.pallas_call(
        functools.partial(_diff_attn_kernel, lam_init=lam_init),
        grid=(b, nh),
        in_specs=[
            pl.BlockSpec((1, 1, s, dv), lambda bi, hi: (bi, hi, 0, 0)),
            pl.BlockSpec((1, 1, s, dv), lambda bi, hi: (bi, hi, 0, 0)),
            pl.BlockSpec((1, s // ATT_K, dv, ATT_K), lambda bi, hi: (bi, 0, hi, 0)),
            pl.BlockSpec((SUBLANES, LANES), lambda bi, hi: (0, 0)),
            pl.BlockSpec((1, dv), lambda bi, hi: (0, 0)),
        ],
        out_specs=pl.BlockSpec((1, 1, s, dv), lambda bi, hi: (bi, hi, 0, 0)),
        out_shape=jax.ShapeDtypeStruct((b, nh, s, dv), jnp.bfloat16),
        scratch_shapes=[
            pltpu.VMEM((1, 2 * ATT_Q), jnp.float32),
            pltpu.VMEM((1, 2 * ATT_Q), jnp.float32),
            pltpu.VMEM((dv, 2 * ATT_Q), jnp.float32),
            pltpu.VMEM((2 * ATT_Q, dv), jnp.bfloat16),
            pltpu.VMEM((ATT_K, 2 * ATT_Q), jnp.float32),
            pltpu.VMEM((ATT_K, 2 * ATT_Q), jnp.float32),
            pltpu.VMEM((ATT_K, 2 * ATT_Q), jnp.float32),
        ],
        compiler_params=pltpu.CompilerParams(
            dimension_semantics=("arbitrary", "arbitrary"), vmem_limit_bytes=VMEM_LIMIT),
        name="diff_attn",
    )(q, k, vt, lamv, g_subln)


def _rms(x, g):
    ms = jnp.mean(x * x, axis=-1, keepdims=True)
    return x * lax.rsqrt(ms + RMS_EPS) * g


def _ffn_out_kernel(x_ref, a_ref, o_ref, wout_ref, gffn_ref, wup_ref, wf_ref, bf_ref, wdn_ref,
                    gfin_ref, y_ref, gate_ref, *, final_norm):
    i = pl.program_id(1)
    rows = x_ref.shape[1]
    o_cat = jnp.concatenate([o_ref[0, hd] for hd in range(N_HEADS)], axis=-1)
    ao = jnp.concatenate([a_ref[0], o_cat], axis=-1)
    x1 = x_ref[0] + _nn_dot(ao, wout_ref[...])

    h = _rms(x1, gffn_ref[...]).astype(jnp.bfloat16)

    @pl.when(i == 0)
    def _():
        gate_ref[0:FFN_HALO, :] = jnp.zeros((FFN_HALO, FFN_DIM), jnp.float32)

    def up_proj(j):
        gate_j = _nn_dot(h, wup_ref[:, j * FFN_BLOCK:(j + 1) * FFN_BLOCK])
        val_j = _nn_dot(h, wup_ref[:, FFN_DIM + j * FFN_BLOCK:FFN_DIM + (j + 1) * FFN_BLOCK])
        return gate_j, val_j

    n_blocks = FFN_DIM // FFN_BLOCK
    x2 = x1
    up_next = up_proj(0)
    for j in range(n_blocks):
        c0 = j * FFN_BLOCK
        gate_j, val_j = up_next
        if j + 1 < n_blocks:
            up_next = up_proj(j + 1)
        gate_ref[FFN_HALO:FFN_HALO + rows, c0:c0 + FFN_BLOCK] = gate_j
        conv = bf_ref[:, c0:c0 + FFN_BLOCK]
        for tap in range(FFN_CONV_KERNEL):
            off = FFN_HALO - (FFN_CONV_KERNEL - 1) + tap
            conv = conv + wf_ref[tap:tap + 1, c0:c0 + FFN_BLOCK] * gate_ref[off:off + rows, c0:c0 + FFN_BLOCK]
        carry_rows = gate_ref[rows:rows + FFN_HALO, c0:c0 + FFN_BLOCK]
        hidden = (conv * _sigmoid(conv) * val_j).astype(jnp.bfloat16)
        x2 = x2 + _nn_dot(hidden, wdn_ref[c0:c0 + FFN_BLOCK, :])
        gate_ref[0:FFN_HALO, c0:c0 + FFN_BLOCK] = carry_rows
    if final_norm:
        x2 = _rms(x2, gfin_ref[...])
    y_ref[0] = x2


def _ffn_out(x, a, o, w_out, g_ffn, w_up, w_fconv, b_fconv, w_down, g_final, final_norm):
    b, s, d = x.shape
    t = FFN_ROWS
    const = lambda bi, i: (0, 0)
    resident = functools.partial(pl.BlockSpec, index_map=const, pipeline_mode=pl.Buffered(1))
    return pl.pallas_call(
        functools.partial(_ffn_out_kernel, final_norm=final_norm),
        grid=(b, s // t),
        in_specs=[
            pl.BlockSpec((1, t, d), lambda bi, i: (bi, i, 0)),
            pl.BlockSpec((1, t, CONV_WIDTH), lambda bi, i: (bi, i, 0)),
            pl.BlockSpec((1, N_HEADS, t, V_DIM), lambda bi, i: (bi, 0, i, 0)),
            resident((d, d)),
            pl.BlockSpec((1, d), const),
            resident((d, 2 * FFN_DIM)),
            pl.BlockSpec((SUBLANES, FFN_DIM), const),
            pl.BlockSpec((1, FFN_DIM), const),
            resident((FFN_DIM, d)),
            pl.BlockSpec((1, d), const),
        ],
        out_specs=pl.BlockSpec((1, t, d), lambda bi, i: (bi, i, 0)),
        out_shape=jax.ShapeDtypeStruct((b, s, d), jnp.float32),
        scratch_shapes=[pltpu.VMEM((FFN_HALO + t, FFN_DIM), jnp.float32)],
        compiler_params=pltpu.CompilerParams(
            dimension_semantics=("arbitrary", "arbitrary"), vmem_limit_bytes=VMEM_LIMIT),
        name="ffn_out",
    )(x, a, o, w_out, g_ffn, w_up, w_fconv, b_fconv, w_down, g_final)


def _rope_tables(seq):
    inv_freq = 1.0 / (ROPE_THETA ** (jnp.arange(0, HEAD_DIM, 2, dtype=jnp.float32) / HEAD_DIM))
    ang = jnp.arange(seq, dtype=jnp.float32)[:, None] * inv_freq[None, :]
    cos, sin = jnp.cos(ang), jnp.sin(ang)
    reps = LANES // HEAD_DIM
    cos_t = jnp.tile(jnp.concatenate([cos, cos], axis=-1), (1, reps))
    sin_t = jnp.tile(jnp.concatenate([-sin, sin], axis=-1), (1, reps))
    return cos_t, sin_t


def _pad_rows(w, rows):
    return jnp.pad(w, ((0, rows - w.shape[0]), (0, 0)))


def kernel(x, g_mix, w_in, w_dw, b_dw, ln_g, ln_b, lambda_q1, lambda_k1, lambda_q2, lambda_k2,
           g_subln, w_out, g_ffn, w_up, w_fconv, b_fconv, w_down, g_final):
    depth = w_in.shape[0]
    seq = x.shape[1]
    cos_t, sin_t = _rope_tables(seq)
    bf = jnp.bfloat16
    row = lambda v: v.reshape(1, -1).astype(jnp.float32)
    for layer in range(depth):
        w_main = w_in[layer][:, :QK_COLS].astype(bf)
        w_vt = w_in[layer][:, QK_COLS:].T.astype(bf)
        a_pre, q, k, vt = _mix_in(x, row(g_mix[layer]), w_main, w_vt, cos_t, sin_t)
        w_taps = jnp.broadcast_to(w_dw[layer].astype(jnp.float32)[:, None, :],
                                  (CONV_KERNEL, SUBLANES, CONV_WIDTH))
        a = _conv_mod(a_pre, w_taps, row(b_dw[layer]), row(ln_g[layer]), row(ln_b[layer]))
        lam_init = 0.8 - 0.6 * math.exp(-0.3 * layer)
        lam_rows = jnp.stack([lambda_q1[layer], lambda_k1[layer], lambda_q2[layer], lambda_k2[layer]])
        lamv = jnp.pad(lam_rows.astype(jnp.float32),
                       ((0, SUBLANES - 4), (0, LANES - lam_rows.shape[1])))
        o = _diff_attn(q, k, vt, lamv, row(g_subln[layer]), lam_init)
        x = _ffn_out(x, a, o, w_out[layer].astype(bf), row(g_ffn[layer]), w_up[layer].astype(bf),
                     _pad_rows(w_fconv[layer], SUBLANES), row(b_fconv[layer]),
                     w_down[layer].astype(bf), row(g_final), final_norm=(layer == depth - 1))
    return x
```

```python
import functools
import math

import jax
import jax.numpy as jnp
from jax import lax
from jax.experimental import pallas as pl
from jax.experimental.pallas import tpu as pltpu

D_MODEL = 1024
CHUNK = 64
CONV_WIDTH = 512
ATTN_WIDTH = 512
N_HEADS = 4
HEAD_DIM = 64
V_DIM = 2 * HEAD_DIM
CONV_KERNEL = 31
FFN_DIM = 2816
FFN_CONV_KERNEL = 3
ROPE_THETA = 10000.0
RMS_EPS = 1e-6
LN_EPS = 1e-5
SUBLN_EPS = 1e-5
QK_COLS = 2 * CONV_WIDTH + 2 * ATTN_WIDTH

LANES = 128
SUBLANES = 8
LOG2E = math.log2(math.e)
NEG_BIG = -1e30

MIX_ROWS = 512
CONV_ROWS = 512
CONV_HALO = 32
CONV_CHUNK = 32
ATT_Q = 512
ATT_K = 512
ATT_ONES = 16
ATT_SAFE = 2.0 ** 60
FFN_ROWS = 256
FFN_HALO = SUBLANES
FFN_BLOCK = 256

VMEM_LIMIT = 56 * 1024 * 1024


def _nt_dot(a, b):
    return lax.dot_general(a, b, (((1,), (1,)), ((), ())), preferred_element_type=jnp.float32)


def _nn_dot(a, b):
    return jnp.dot(a, b, preferred_element_type=jnp.float32)


def _sigmoid(x):
    return 1.0 / (1.0 + jnp.exp(-x))


def _mix_in_kernel(x_ref, g_ref, w_ref, wvt_ref, cos_ref, sin_ref,
                   apre_ref, q_ref, k_ref, vt_ref, *, q_scale):
    x = x_ref[0]
    ms = jnp.mean(x * x, axis=-1, keepdims=True)
    h = (x * lax.rsqrt(ms + RMS_EPS) * g_ref[...]).astype(jnp.bfloat16)

    main = _nn_dot(h, w_ref[...])
    c_val = main[:, :CONV_WIDTH]
    c_gate = main[:, CONV_WIDTH:2 * CONV_WIDTH]
    apre_ref[0] = c_val * _sigmoid(c_gate)

    cos = cos_ref[...]
    sin = sin_ref[...]
    lane = lax.broadcasted_iota(jnp.int32, cos.shape, 1)
    first_half = (lane % HEAD_DIM) < (HEAD_DIM // 2)

    def rope(t):
        up = pltpu.roll(t, LANES - HEAD_DIM // 2, axis=1)
        down = pltpu.roll(t, HEAD_DIM // 2, axis=1)
        return t * cos + jnp.where(first_half, up, down) * sin

    q0 = 2 * CONV_WIDTH
    k0 = q0 + ATTN_WIDTH
    for hd in range(N_HEADS):
        qh = main[:, q0 + hd * V_DIM:q0 + (hd + 1) * V_DIM]
        kh = main[:, k0 + hd * V_DIM:k0 + (hd + 1) * V_DIM]
        q_ref[0, hd] = (rope(qh) * q_scale).astype(jnp.bfloat16)
        k_ref[0, hd] = rope(kh).astype(jnp.bfloat16)

    vt = _nt_dot(wvt_ref[...], h).astype(jnp.bfloat16)
    rows = x.shape[0]
    ones = jnp.ones((ATT_ONES, ATT_K), jnp.bfloat16)
    for j in range(rows // ATT_K):
        for hd in range(N_HEADS):
            vt_ref[0, j, hd, 0:V_DIM, :] = vt[hd * V_DIM:(hd + 1) * V_DIM, j * ATT_K:(j + 1) * ATT_K]
            vt_ref[0, j, hd, V_DIM:V_DIM + ATT_ONES, :] = ones


def _mix_in(x, g_mix, w_main, w_vt, cos_t, sin_t):
    b, s, d = x.shape
    t = MIX_ROWS
    q_scale = HEAD_DIM ** -0.5 * LOG2E
    const = lambda bi, i: (0, 0)
    return pl.pallas_call(
        functools.partial(_mix_in_kernel, q_scale=q_scale),
        grid=(b, s // t),
        in_specs=[
            pl.BlockSpec((1, t, d), lambda bi, i: (bi, i, 0)),
            pl.BlockSpec((1, d), const),
            pl.BlockSpec((d, QK_COLS), const),
            pl.BlockSpec((ATTN_WIDTH, d), const),
            pl.BlockSpec((t, LANES), lambda bi, i: (i, 0)),
            pl.BlockSpec((t, LANES), lambda bi, i: (i, 0)),
        ],
        out_specs=[
            pl.BlockSpec((1, t, CONV_WIDTH), lambda bi, i: (bi, i, 0)),
            pl.BlockSpec((1, N_HEADS, t, V_DIM), lambda bi, i: (bi, 0, i, 0)),
            pl.BlockSpec((1, N_HEADS, t, V_DIM), lambda bi, i: (bi, 0, i, 0)),
            pl.BlockSpec((1, t // ATT_K, N_HEADS, V_DIM + ATT_ONES, ATT_K),
                         lambda bi, i: (bi, i, 0, 0, 0)),
        ],
        out_shape=[
            jax.ShapeDtypeStruct((b, s, CONV_WIDTH), jnp.float32),
            jax.ShapeDtypeStruct((b, N_HEADS, s, V_DIM), jnp.bfloat16),
            jax.ShapeDtypeStruct((b, N_HEADS, s, V_DIM), jnp.bfloat16),
            jax.ShapeDtypeStruct((b, s // ATT_K, N_HEADS, V_DIM + ATT_ONES, ATT_K), jnp.bfloat16),
        ],
        compiler_params=pltpu.CompilerParams(
            dimension_semantics=("arbitrary", "arbitrary"), vmem_limit_bytes=VMEM_LIMIT),
        name="mix_in",
    )(x, g_mix, w_main, w_vt, cos_t, sin_t)


def _conv_mod_kernel(cur_ref, halo_ref, w_ref, b_ref, lng_ref, lnb_ref, a_ref, u_ref):
    i = pl.program_id(1)
    rows = cur_ref.shape[1]
    halo = halo_ref[0]
    u_ref[0, 0:CONV_HALO, :] = jnp.where(i == 0, jnp.zeros_like(halo), halo)
    u_ref[0, CONV_HALO:CONV_HALO + rows, :] = cur_ref[0]
    span = CONV_HALO + rows - SUBLANES
    for r in range(1, SUBLANES):
        u_ref[r, 0:span, :] = u_ref[0, r:r + span, :]

    bias = b_ref[...]
    lng = lng_ref[...]
    lnb = lnb_ref[...]
    first_tap = CONV_HALO - (CONV_KERNEL - 1)

    groups = CONV_CHUNK // SUBLANES

    def chunk(c, carry):
        base = c * CONV_CHUNK
        accs = [jnp.broadcast_to(bias, (CONV_CHUNK, CONV_WIDTH)), None]
        for tap in range(CONV_KERNEL):
            off = first_tap + tap
            aligned = base + (off // SUBLANES) * SUBLANES
            w_tap = jnp.concatenate([w_ref[tap]] * groups, axis=0)
            term = w_tap * u_ref[off % SUBLANES, pl.ds(aligned, CONV_CHUNK), :]
            accs[tap % 2] = term if accs[tap % 2] is None else accs[tap % 2] + term
        acc = accs[0] + accs[1]
        mu = jnp.mean(acc, axis=-1, keepdims=True)
        cen = acc - mu
        var = jnp.mean(cen * cen, axis=-1, keepdims=True)
        y = cen * lax.rsqrt(var + LN_EPS) * lng + lnb
        a_ref[0, pl.ds(base, CONV_CHUNK), :] = (y * _sigmoid(y)).astype(a_ref.dtype)
        return carry

    for c in range(rows // CONV_CHUNK):
        chunk(c, 0)


def _conv_mod(a_pre, w_dw, b_dw, ln_g, ln_b):
    b, s, c = a_pre.shape
    t = CONV_ROWS
    per = t // CONV_HALO
    const = lambda bi, i: (0, 0)
    return pl.pallas_call(
        _conv_mod_kernel,
        grid=(b, s // t),
        in_specs=[
            pl.BlockSpec((1, t, c), lambda bi, i: (bi, i, 0)),
            pl.BlockSpec((1, CONV_HALO, c), lambda bi, i: (bi, jnp.maximum(i * per - 1, 0), 0)),
            pl.BlockSpec((CONV_KERNEL, SUBLANES, c), lambda bi, i: (0, 0, 0)),
            pl.BlockSpec((1, c), const),
            pl.BlockSpec((1, c), const),
            pl.BlockSpec((1, c), const),
        ],
        out_specs=pl.BlockSpec((1, t, c), lambda bi, i: (bi, i, 0)),
        out_shape=jax.ShapeDtypeStruct((b, s, c), jnp.bfloat16),
        scratch_shapes=[pltpu.VMEM((SUBLANES, CONV_HALO + t, c), jnp.float32)],
        compiler_params=pltpu.CompilerParams(
            dimension_semantics=("arbitrary", "arbitrary"), vmem_limit_bytes=VMEM_LIMIT),
        name="conv_mod",
    )(a_pre, a_pre, w_dw, b_dw, ln_g, ln_b)


def _diff_attn_kernel(q_ref, k_ref, vt_ref, lamv_ref, g_ref, o_ref,
                      m_ref, acc_ref, qq_ref, sa_ref, sb_ref, mask_ref, *, lam_init):
    seq = q_ref.shape[2]
    tq, tk = ATT_Q, ATT_K
    lamv = lamv_ref[...]
    lam = (jnp.exp(jnp.sum(lamv[0:1] * lamv[1:2], axis=-1, keepdims=True))
           - jnp.exp(jnp.sum(lamv[2:3] * lamv[3:4], axis=-1, keepdims=True)) + lam_init)
    gain = g_ref[...] * (1.0 - lam_init)

    lane = lax.broadcasted_iota(jnp.int32, (tq, V_DIM), 1)
    key_chunk = lax.broadcasted_iota(jnp.int32, (tk, 2 * tq), 0) // CHUNK
    qry_chunk = (lax.broadcasted_iota(jnp.int32, (tk, 2 * tq), 1) % tq) // CHUNK
    mask_ref[...] = jnp.where(key_chunk <= qry_chunk, 0.0, NEG_BIG)

    def scores(ki, s_ref):
        k0 = pl.multiple_of(ki * tk, tk)
        s_ref[...] = _nt_dot(k_ref[0, 0, pl.ds(k0, tk), :], qq_ref[...])

    def values(ki, p):
        return _nn_dot(vt_ref[0, ki, 0], p.astype(jnp.bfloat16))

    def anchor(qi, s_ref):
        s = s_ref[...] + mask_ref[...]
        m = jnp.max(s, axis=0, keepdims=True)
        m_ref[...] = m
        acc_ref[...] = values(qi, jnp.exp2(s - m))

    def add_tile(ki, s_ref):
        acc_ref[...] += values(ki, jnp.exp2(s_ref[...] - m_ref[...]))

    def online_tile(ki, s_ref, masked):
        s = s_ref[...]
        if masked:
            s = s + mask_ref[...]
        m_old = m_ref[...]
        m_new = jnp.maximum(m_old, jnp.max(s, axis=0, keepdims=True))
        acc_ref[...] = jnp.exp2(m_old - m_new) * acc_ref[...] + values(ki, jnp.exp2(s - m_new))
        m_ref[...] = m_new

    def q_tile(qi, carry):
        q0 = pl.multiple_of(qi * tq, tq)
        qt = q_ref[0, 0, pl.ds(q0, tq), :]
        zero = jnp.zeros_like(qt)
        qq_ref[0:tq, :] = jnp.where(lane < HEAD_DIM, qt, zero)
        qq_ref[tq:2 * tq, :] = jnp.where(lane >= HEAD_DIM, qt, zero)

        scores(qi, sb_ref)
        scores(0, sa_ref)
        anchor(qi, sb_ref)

        def pair(i, c):
            scores(2 * i + 1, sb_ref)
            add_tile(2 * i, sa_ref)
            scores(2 * i + 2, sa_ref)
            add_tile(2 * i + 1, sb_ref)
            return c

        lax.fori_loop(0, (qi - 1) // 2, pair, 0)

        @pl.when(qi % 2 == 1)
        def _():
            add_tile(qi - 1, sa_ref)

        @pl.when(jnp.logical_and(qi % 2 == 0, qi > 0))
        def _():
            scores(qi - 1, sb_ref)
            add_tile(qi - 2, sa_ref)
            add_tile(qi - 1, sb_ref)

        denom = acc_ref[V_DIM:V_DIM + 1, :]
        overflowed = jnp.max(jnp.where(denom < ATT_SAFE, 0.0, 1.0)) > 0.5

        @pl.when(overflowed)
        def _():
            m_ref[...] = jnp.full(m_ref.shape, NEG_BIG, jnp.float32)
            acc_ref[...] = jnp.zeros(acc_ref.shape, jnp.float32)

            def step(ki, c):
                scores(ki, sa_ref)
                online_tile(ki, sa_ref, False)
                return c

            lax.fori_loop(0, qi, step, 0)
            scores(qi, sa_ref)
            online_tile(qi, sa_ref, True)

        acc = acc_ref[...]
        inv_l = 1.0 / acc[V_DIM:V_DIM + 1, :]
        o_t = (acc[:V_DIM, :tq] * inv_l[:, :tq]
               - lam * (acc[:V_DIM, tq:] * inv_l[:, tq:]))
        o = o_t.T
        ms = jnp.mean(o * o, axis=-1, keepdims=True)
        o_ref[0, 0, pl.ds(q0, tq), :] = (o * lax.rsqrt(ms + SUBLN_EPS) * gain).astype(o_ref.dtype)
        return carry

    lax.fori_loop(0, seq // tq, q_tile, 0)


def _diff_attn(q, k, vt, lamv, g_subln, lam_init):
    b, nh, s, dv = q.shape
    return pl.pallas_call(
        functools.partial(_diff_attn_kernel, lam_init=lam_init),
        grid=(b, nh),
        in_specs=[
            pl.BlockSpec((1, 1, s, dv), lambda bi, hi: (bi, hi, 0, 0)),
            pl.BlockSpec((1, 1, s, dv), lambda bi, hi: (bi, hi, 0, 0)),
            pl.BlockSpec((1, s // ATT_K, 1, dv + ATT_ONES, ATT_K), lambda bi, hi: (bi, 0, hi, 0, 0)),
            pl.BlockSpec((SUBLANES, LANES), lambda bi, hi: (0, 0)),
            pl.BlockSpec((1, dv), lambda bi, hi: (0, 0)),
        ],
        out_specs=pl.BlockSpec((1, 1, s, dv), lambda bi, hi: (bi, hi, 0, 0)),
        out_shape=jax.ShapeDtypeStruct((b, nh, s, dv), jnp.bfloat16),
        scratch_shapes=[
            pltpu.VMEM((1, 2 * ATT_Q), jnp.float32),
            pltpu.VMEM((dv + ATT_ONES, 2 * ATT_Q), jnp.float32),
            pltpu.VMEM((2 * ATT_Q, dv), jnp.bfloat16),
            pltpu.VMEM((ATT_K, 2 * ATT_Q), jnp.float32),
            pltpu.VMEM((ATT_K, 2 * ATT_Q), jnp.float32),
            pltpu.VMEM((ATT_K, 2 * ATT_Q), jnp.float32),
        ],
        compiler_params=pltpu.CompilerParams(
            dimension_semantics=("arbitrary", "arbitrary"), vmem_limit_bytes=VMEM_LIMIT),
        name="diff_attn",
    )(q, k, vt, lamv, g_subln)


def _rms(x, g):
    ms = jnp.mean(x * x, axis=-1, keepdims=True)
    return x * lax.rsqrt(ms + RMS_EPS) * g


def _ffn_out_kernel(x_ref, a_ref, o_ref, wout_ref, gffn_ref, wup_ref, wf_ref, bf_ref, wdn_ref,
                    gfin_ref, y_ref, gate_ref, *, final_norm):
    i = pl.program_id(1)
    rows = x_ref.shape[1]
    o_cat = jnp.concatenate([o_ref[0, hd] for hd in range(N_HEADS)], axis=-1)
    ao = jnp.concatenate([a_ref[0], o_cat], axis=-1)
    x1 = x_ref[0] + _nn_dot(ao, wout_ref[...])

    h = _rms(x1, gffn_ref[...]).astype(jnp.bfloat16)

    @pl.when(i == 0)
    def _():
        gate_ref[0:FFN_HALO, :] = jnp.zeros((FFN_HALO, FFN_DIM), jnp.float32)

    def up_proj(j):
        gate_j = _nn_dot(h, wup_ref[:, j * FFN_BLOCK:(j + 1) * FFN_BLOCK])
        val_j = _nn_dot(h, wup_ref[:, FFN_DIM + j * FFN_BLOCK:FFN_DIM + (j + 1) * FFN_BLOCK])
        return gate_j, val_j

    n_blocks = FFN_DIM // FFN_BLOCK
    x2 = x1
    up_next = up_proj(0)
    for j in range(n_blocks):
        c0 = j * FFN_BLOCK
        gate_j, val_j = up_next
        if j + 1 < n_blocks:
            up_next = up_proj(j + 1)
        gate_ref[FFN_HALO:FFN_HALO + rows, c0:c0 + FFN_BLOCK] = gate_j
        conv = bf_ref[:, c0:c0 + FFN_BLOCK]
        for tap in range(FFN_CONV_KERNEL):
            off = FFN_HALO - (FFN_CONV_KERNEL - 1) + tap
            conv = conv + wf_ref[tap:tap + 1, c0:c0 + FFN_BLOCK] * gate_ref[off:off + rows, c0:c0 + FFN_BLOCK]
        carry_rows = gate_ref[rows:rows + FFN_HALO, c0:c0 + FFN_BLOCK]
        hidden = (conv * _sigmoid(conv) * val_j).astype(jnp.bfloat16)
        x2 = x2 + _nn_dot(hidden, wdn_ref[c0:c0 + FFN_BLOCK, :])
        gate_ref[0:FFN_HALO, c0:c0 + FFN_BLOCK] = carry_rows
    if final_norm:
        x2 = _rms(x2, gfin_ref[...])
    y_ref[0] = x2


def _ffn_out(x, a, o, w_out, g_ffn, w_up, w_fconv, b_fconv, w_down, g_final, final_norm):
    b, s, d = x.shape
    t = FFN_ROWS
    const = lambda bi, i: (0, 0)
    resident = functools.partial(pl.BlockSpec, index_map=const, pipeline_mode=pl.Buffered(1))
    return pl.pallas_call(
        functools.partial(_ffn_out_kernel, final_norm=final_norm),
        grid=(b, s // t),
        in_specs=[
            pl.BlockSpec((1, t, d), lambda bi, i: (bi, i, 0)),
            pl.BlockSpec((1, t, CONV_WIDTH), lambda bi, i: (bi, i, 0)),
            pl.BlockSpec((1, N_HEADS, t, V_DIM), lambda bi, i: (bi, 0, i, 0)),
            resident((d, d)),
            pl.BlockSpec((1, d), const),
            resident((d, 2 * FFN_DIM)),
            pl.BlockSpec((SUBLANES, FFN_DIM), const),
            pl.BlockSpec((1, FFN_DIM), const),
            resident((FFN_DIM, d)),
            pl.BlockSpec((1, d), const),
        ],
        out_specs=pl.BlockSpec((1, t, d), lambda bi, i: (bi, i, 0)),
        out_shape=jax.ShapeDtypeStruct((b, s, d), jnp.float32),
        scratch_shapes=[pltpu.VMEM((FFN_HALO + t, FFN_DIM), jnp.float32)],
        compiler_params=pltpu.CompilerParams(
            dimension_semantics=("arbitrary", "arbitrary"), vmem_limit_bytes=VMEM_LIMIT),
        name="ffn_out",
    )(x, a, o, w_out, g_ffn, w_up, w_fconv, b_fconv, w_down, g_final)


def _rope_tables(seq):
    inv_freq = 1.0 / (ROPE_THETA ** (jnp.arange(0, HEAD_DIM, 2, dtype=jnp.float32) / HEAD_DIM))
    ang = jnp.arange(seq, dtype=jnp.float32)[:, None] * inv_freq[None, :]
    cos, sin = jnp.cos(ang), jnp.sin(ang)
    reps = LANES // HEAD_DIM
    cos_t = jnp.tile(jnp.concatenate([cos, cos], axis=-1), (1, reps))
    sin_t = jnp.tile(jnp.concatenate([-sin, sin], axis=-1), (1, reps))
    return cos_t, sin_t


def _pad_rows(w, rows):
    return jnp.pad(w, ((0, rows - w.shape[0]), (0, 0)))


def kernel(x, g_mix, w_in, w_dw, b_dw, ln_g, ln_b, lambda_q1, lambda_k1, lambda_q2, lambda_k2,
           g_subln, w_out, g_ffn, w_up, w_fconv, b_fconv, w_down, g_final):
    depth = w_in.shape[0]
    seq = x.shape[1]
    cos_t, sin_t = _rope_tables(seq)
    bf = jnp.bfloat16
    row = lambda v: v.reshape(1, -1).astype(jnp.float32)
    for layer in range(depth):
        w_main = w_in[layer][:, :QK_COLS].astype(bf)
        w_vt = w_in[layer][:, QK_COLS:].T.astype(bf)
        a_pre, q, k, vt = _mix_in(x, row(g_mix[layer]), w_main, w_vt, cos_t, sin_t)
        w_taps = jnp.broadcast_to(w_dw[layer].astype(jnp.float32)[:, None, :],
                                  (CONV_KERNEL, SUBLANES, CONV_WIDTH))
        a = _conv_mod(a_pre, w_taps, row(b_dw[layer]), row(ln_g[layer]), row(ln_b[layer]))
        lam_init = 0.8 - 0.6 * math.exp(-0.3 * layer)
        lam_rows = jnp.stack([lambda_q1[layer], lambda_k1[layer], lambda_q2[layer], lambda_k2[layer]])
        lamv = jnp.pad(lam_rows.astype(jnp.float32),
                       ((0, SUBLANES - 4), (0, LANES - lam_rows.shape[1])))
        o = _diff_attn(q, k, vt, lamv, row(g_subln[layer]), lam_init)
        x = _ffn_out(x, a, o, w_out[layer].astype(bf), row(g_ffn[layer]), w_up[layer].astype(bf),
                     _pad_rows(w_fconv[layer], SUBLANES), row(b_fconv[layer]),
                     w_down[layer].astype(bf), row(g_final), final_norm=(layer == depth - 1))
    return x
```

```python
import functools
import math

import jax
import jax.numpy as jnp
from jax import lax
from jax.experimental import pallas as pl
from jax.experimental.pallas import tpu as pltpu

D_MODEL = 1024
CHUNK = 64
CONV_WIDTH = 512
ATTN_WIDTH = 512
N_HEADS = 4
HEAD_DIM = 64
V_DIM = 2 * HEAD_DIM
CONV_KERNEL = 31
FFN_DIM = 2816
FFN_CONV_KERNEL = 3
ROPE_THETA = 10000.0
RMS_EPS = 1e-6
LN_EPS = 1e-5
SUBLN_EPS = 1e-5
QK_COLS = 2 * CONV_WIDTH + 2 * ATTN_WIDTH

LANES = 128
SUBLANES = 8
LOG2E = math.log2(math.e)
NEG_BIG = -1e30

MIX_ROWS = 512
CONV_ROWS = 512
CONV_HALO = 32
CONV_CHUNK = 32
ATT_Q = 512
ATT_K = 512
ATT_ONES = 16
ATT_SAFE = 2.0 ** 60
FFN_ROWS = 256
FFN_HALO = SUBLANES
FFN_BLOCK = 256

VMEM_LIMIT = 56 * 1024 * 1024


def _nt_dot(a, b):
    return lax.dot_general(a, b, (((1,), (1,)), ((), ())), preferred_element_type=jnp.float32)


def _nn_dot(a, b):
    return jnp.dot(a, b, preferred_element_type=jnp.float32)


def _sigmoid(x):
    return 1.0 / (1.0 + jnp.exp(-x))


def _mix_in_kernel(x_ref, g_ref, w_ref, wvt_ref, cos_ref, sin_ref,
                   apre_ref, q_ref, k_ref, vt_ref, *, q_scale):
    x = x_ref[0]
    ms = jnp.mean(x * x, axis=-1, keepdims=True)
    h = (x * lax.rsqrt(ms + RMS_EPS) * g_ref[...]).astype(jnp.bfloat16)

    main = _nn_dot(h, w_ref[...])
    c_val = main[:, :CONV_WIDTH]
    c_gate = main[:, CONV_WIDTH:2 * CONV_WIDTH]
    apre_ref[0] = c_val * _sigmoid(c_gate)

    cos = cos_ref[...]
    sin = sin_ref[...]
    lane = lax.broadcasted_iota(jnp.int32, cos.shape, 1)
    first_half = (lane % HEAD_DIM) < (HEAD_DIM // 2)

    def rope(t):
        up = pltpu.roll(t, LANES - HEAD_DIM // 2, axis=1)
        down = pltpu.roll(t, HEAD_DIM // 2, axis=1)
        return t * cos + jnp.where(first_half, up, down) * sin

    q0 = 2 * CONV_WIDTH
    k0 = q0 + ATTN_WIDTH
    for hd in range(N_HEADS):
        qh = main[:, q0 + hd * V_DIM:q0 + (hd + 1) * V_DIM]
        kh = main[:, k0 + hd * V_DIM:k0 + (hd + 1) * V_DIM]
        q_ref[0, hd] = (rope(qh) * q_scale).astype(jnp.bfloat16)
        k_ref[0, hd] = rope(kh).astype(jnp.bfloat16)

    vt = _nt_dot(wvt_ref[...], h).astype(jnp.bfloat16)
    rows = x.shape[0]
    ones = jnp.ones((ATT_ONES, ATT_K), jnp.bfloat16)
    for j in range(rows // ATT_K):
        for hd in range(N_HEADS):
            vt_ref[0, j, hd, 0:V_DIM, :] = vt[hd * V_DIM:(hd + 1) * V_DIM, j * ATT_K:(j + 1) * ATT_K]
            vt_ref[0, j, hd, V_DIM:V_DIM + ATT_ONES, :] = ones


def _mix_in(x, g_mix, w_main, w_vt, cos_t, sin_t):
    b, s, d = x.shape
    t = MIX_ROWS
    q_scale = HEAD_DIM ** -0.5 * LOG2E
    const = lambda bi, i: (0, 0)
    return pl.pallas_call(
        functools.partial(_mix_in_kernel, q_scale=q_scale),
        grid=(b, s // t),
        in_specs=[
            pl.BlockSpec((1, t, d), lambda bi, i: (bi, i, 0)),
            pl.BlockSpec((1, d), const),
            pl.BlockSpec((d, QK_COLS), const),
            pl.BlockSpec((ATTN_WIDTH, d), const),
            pl.BlockSpec((t, LANES), lambda bi, i: (i, 0)),
            pl.BlockSpec((t, LANES), lambda bi, i: (i, 0)),
        ],
        out_specs=[
            pl.BlockSpec((1, t, CONV_WIDTH), lambda bi, i: (bi, i, 0)),
            pl.BlockSpec((1, N_HEADS, t, V_DIM), lambda bi, i: (bi, 0, i, 0)),
            pl.BlockSpec((1, N_HEADS, t, V_DIM), lambda bi, i: (bi, 0, i, 0)),
            pl.BlockSpec((1, t // ATT_K, N_HEADS, V_DIM + ATT_ONES, ATT_K),
                         lambda bi, i: (bi, i, 0, 0, 0)),
        ],
        out_shape=[
            jax.ShapeDtypeStruct((b, s, CONV_WIDTH), jnp.float32),
            jax.ShapeDtypeStruct((b, N_HEADS, s, V_DIM), jnp.bfloat16),
            jax.ShapeDtypeStruct((b, N_HEADS, s, V_DIM), jnp.bfloat16),
            jax.ShapeDtypeStruct((b, s // ATT_K, N_HEADS, V_DIM + ATT_ONES, ATT_K), jnp.bfloat16),
        ],
        compiler_params=pltpu.CompilerParams(
            dimension_semantics=("arbitrary", "arbitrary"), vmem_limit_bytes=VMEM_LIMIT),
        name="mix_in",
    )(x, g_mix, w_main, w_vt, cos_t, sin_t)


def _conv_mod_kernel(cur_ref, halo_ref, w_ref, b_ref, lng_ref, lnb_ref, a_ref, u_ref):
    i = pl.program_id(1)
    rows = cur_ref.shape[1]
    halo = halo_ref[0]
    u_ref[0, 0:CONV_HALO, :] = jnp.where(i == 0, jnp.zeros_like(halo), halo)
    u_ref[0, CONV_HALO:CONV_HALO + rows, :] = cur_ref[0]
    span = CONV_HALO + rows - SUBLANES
    for r in range(1, SUBLANES):
        u_ref[r, 0:span, :] = u_ref[0, r:r + span, :]

    bias = b_ref[...]
    lng = lng_ref[...]
    lnb = lnb_ref[...]
    first_tap = CONV_HALO - (CONV_KERNEL - 1)

    groups = CONV_CHUNK // SUBLANES

    def chunk(c, carry):
        base = c * CONV_CHUNK
        accs = [jnp.broadcast_to(bias, (CONV_CHUNK, CONV_WIDTH)), None]
        for tap in range(CONV_KERNEL):
            off = first_tap + tap
            aligned = base + (off // SUBLANES) * SUBLANES
            w_tap = jnp.concatenate([w_ref[tap]] * groups, axis=0)
            term = w_tap * u_ref[off % SUBLANES, pl.ds(aligned, CONV_CHUNK), :]
            accs[tap % 2] = term if accs[tap % 2] is None else accs[tap % 2] + term
        acc = accs[0] + accs[1]
        mu = jnp.mean(acc, axis=-1, keepdims=True)
        cen = acc - mu
        var = jnp.mean(cen * cen, axis=-1, keepdims=True)
        y = cen * lax.rsqrt(var + LN_EPS) * lng + lnb
        a_ref[0, pl.ds(base, CONV_CHUNK), :] = (y * _sigmoid(y)).astype(a_ref.dtype)
        return carry

    for c in range(rows // CONV_CHUNK):
        chunk(c, 0)


def _conv_mod(a_pre, w_dw, b_dw, ln_g, ln_b):
    b, s, c = a_pre.shape
    t = CONV_ROWS
    per = t // CONV_HALO
    const = lambda bi, i: (0, 0)
    return pl.pallas_call(
        _conv_mod_kernel,
        grid=(b, s // t),
        in_specs=[
            pl.BlockSpec((1, t, c), lambda bi, i: (bi, i, 0)),
            pl.BlockSpec((1, CONV_HALO, c), lambda bi, i: (bi, jnp.maximum(i * per - 1, 0), 0)),
            pl.BlockSpec((CONV_KERNEL, SUBLANES, c), lambda bi, i: (0, 0, 0)),
            pl.BlockSpec((1, c), const),
            pl.BlockSpec((1, c), const),
            pl.BlockSpec((1, c), const),
        ],
        out_specs=pl.BlockSpec((1, t, c), lambda bi, i: (bi, i, 0)),
        out_shape=jax.ShapeDtypeStruct((b, s, c), jnp.bfloat16),
        scratch_shapes=[pltpu.VMEM((SUBLANES, CONV_HALO + t, c), jnp.float32)],
        compiler_params=pltpu.CompilerParams(
            dimension_semantics=("arbitrary", "arbitrary"), vmem_limit_bytes=VMEM_LIMIT),
        name="conv_mod",
    )(a_pre, a_pre, w_dw, b_dw, ln_g, ln_b)


def _diff_attn_kernel(q_ref, k_ref, vt_ref, lamv_ref, g_ref, o_ref,
                      m_ref, acc_ref, qq_ref, sa_ref, sb_ref, mask_ref, *, lam_init):
    seq = q_ref.shape[2]
    tq, tk = ATT_Q, ATT_K
    lamv = lamv_ref[...]
    lam = (jnp.exp(jnp.sum(lamv[0:1] * lamv[1:2], axis=-1, keepdims=True))
           - jnp.exp(jnp.sum(lamv[2:3] * lamv[3:4], axis=-1, keepdims=True)) + lam_init)
    gain = g_ref[...] * (1.0 - lam_init)

    lane = lax.broadcasted_iota(jnp.int32, (tq, V_DIM), 1)
    key_chunk = lax.broadcasted_iota(jnp.int32, (tk, 2 * tq), 0) // CHUNK
    qry_chunk = (lax.broadcasted_iota(jnp.int32, (tk, 2 * tq), 1) % tq) // CHUNK
    mask_ref[...] = jnp.where(key_chunk <= qry_chunk, 0.0, NEG_BIG)

    def scores(ki, s_ref):
        k0 = pl.multiple_of(ki * tk, tk)
        s_ref[...] = _nt_dot(k_ref[0, 0, pl.ds(k0, tk), :], qq_ref[...])

    def values(ki, p):
        return _nn_dot(vt_ref[0, ki, 0], p.astype(jnp.bfloat16))

    def anchor(qi, s_ref):
        m_ref[...] = jnp.max(s_ref[0:CHUNK, :], axis=0, keepdims=True)
        acc_ref[...] = values(qi, jnp.exp2(s_ref[...] + mask_ref[...] - m_ref[...]))

    def add_tile(ki, s_ref):
        acc_ref[...] += values(ki, jnp.exp2(s_ref[...] - m_ref[...]))

    def online_tile(ki, s_ref, masked):
        s = s_ref[...]
        if masked:
            s = s + mask_ref[...]
        m_old = m_ref[...]
        m_new = jnp.maximum(m_old, jnp.max(s, axis=0, keepdims=True))
        acc_ref[...] = jnp.exp2(m_old - m_new) * acc_ref[...] + values(ki, jnp.exp2(s - m_new))
        m_ref[...] = m_new

    def settle(tile):
        denom = acc_ref[V_DIM:V_DIM + 1, :]
        in_range = jnp.logical_and(denom < ATT_SAFE, denom > 1.0 / ATT_SAFE)
        overflowed = jnp.max(jnp.where(in_range, 0.0, 1.0)) > 0.5

        @pl.when(overflowed)
        def _():
            m_ref[...] = jnp.full(m_ref.shape, NEG_BIG, jnp.float32)
            acc_ref[...] = jnp.zeros(acc_ref.shape, jnp.float32)

            def step(ki, c):
                scores(ki, sa_ref)
                online_tile(ki, sa_ref, False)
                return c

            lax.fori_loop(0, tile, step, 0)
            scores(tile, sa_ref)
            online_tile(tile, sa_ref, True)

    def emit(tile):
        acc = acc_ref[...]
        inv_l = 1.0 / acc[V_DIM:V_DIM + 1, :]
        o_t = (acc[:V_DIM, :tq] * inv_l[:, :tq]
               - lam * (acc[:V_DIM, tq:] * inv_l[:, tq:]))
        o = o_t.T
        ms = jnp.mean(o * o, axis=-1, keepdims=True)
        r0 = pl.multiple_of(tile * tq, tq)
        o_ref[0, 0, pl.ds(r0, tq), :] = (o * lax.rsqrt(ms + SUBLN_EPS) * gain).astype(o_ref.dtype)

    def q_tile(qi, carry):
        prev = jnp.maximum(qi - 1, 0)
        settle(prev)
        emit(prev)

        q0 = pl.multiple_of(qi * tq, tq)
        qt = q_ref[0, 0, pl.ds(q0, tq), :]
        zero = jnp.zeros_like(qt)
        qq_ref[0:tq, :] = jnp.where(lane < HEAD_DIM, qt, zero)
        qq_ref[tq:2 * tq, :] = jnp.where(lane >= HEAD_DIM, qt, zero)

        scores(qi, sb_ref)
        scores(0, sa_ref)
        anchor(qi, sb_ref)

        def pair(i, c):
            scores(2 * i + 1, sb_ref)
            add_tile(2 * i, sa_ref)
            scores(2 * i + 2, sa_ref)
            add_tile(2 * i + 1, sb_ref)
            return c

        lax.fori_loop(0, (qi - 1) // 2, pair, 0)

        @pl.when(qi % 2 == 1)
        def _():
            add_tile(qi - 1, sa_ref)

        @pl.when(jnp.logical_and(qi % 2 == 0, qi > 0))
        def _():
            scores(qi - 1, sb_ref)
            add_tile(qi - 2, sa_ref)
            add_tile(qi - 1, sb_ref)

        return carry

    n_q = seq // tq
    acc_ref[...] = jnp.ones(acc_ref.shape, jnp.float32)
    lax.fori_loop(0, n_q, q_tile, 0)
    settle(n_q - 1)
    emit(n_q - 1)


def _diff_attn(q, k, vt, lamv, g_subln, lam_init):
    b, nh, s, dv = q.shape
    return pl.pallas_call(
        functools.partial(_diff_attn_kernel, lam_init=lam_init),
        grid=(b, nh),
        in_specs=[
            pl.BlockSpec((1, 1, s, dv), lambda bi, hi: (bi, hi, 0, 0)),
            pl.BlockSpec((1, 1, s, dv), lambda bi, hi: (bi, hi, 0, 0)),
            pl.BlockSpec((1, s // ATT_K, 1, dv + ATT_ONES, ATT_K), lambda bi, hi: (bi, 0, hi, 0, 0)),
            pl.BlockSpec((SUBLANES, LANES), lambda bi, hi: (0, 0)),
            pl.BlockSpec((1, dv), lambda bi, hi: (0, 0)),
        ],
        out_specs=pl.BlockSpec((1, 1, s, dv), lambda bi, hi: (bi, hi, 0, 0)),
        out_shape=jax.ShapeDtypeStruct((b, nh, s, dv), jnp.bfloat16),
        scratch_shapes=[
            pltpu.VMEM((1, 2 * ATT_Q), jnp.float32),
            pltpu.VMEM((dv + ATT_ONES, 2 * ATT_Q), jnp.float32),
            pltpu.VMEM((2 * ATT_Q, dv), jnp.bfloat16),
            pltpu.VMEM((ATT_K, 2 * ATT_Q), jnp.float32),
            pltpu.VMEM((ATT_K, 2 * ATT_Q), jnp.float32),
            pltpu.VMEM((ATT_K, 2 * ATT_Q), jnp.float32),
        ],
        compiler_params=pltpu.CompilerParams(
            dimension_semantics=("arbitrary", "arbitrary"), vmem_limit_bytes=VMEM_LIMIT),
        name="diff_attn",
    )(q, k, vt, lamv, g_subln)


def _rms(x, g):
    ms = jnp.mean(x * x, axis=-1, keepdims=True)
    return x * lax.rsqrt(ms + RMS_EPS) * g


def _ffn_out_kernel(x_ref, a_ref, o_ref, wout_ref, gffn_ref, wup_ref, wf_ref, bf_ref, wdn_ref,
                    gfin_ref, y_ref, gate_ref, *, final_norm):
    i = pl.program_id(1)
    rows = x_ref.shape[1]
    o_cat = jnp.concatenate([o_ref[0, hd] for hd in range(N_HEADS)], axis=-1)
    ao = jnp.concatenate([a_ref[0], o_cat], axis=-1)
    x1 = x_ref[0] + _nn_dot(ao, wout_ref[...])

    h = _rms(x1, gffn_ref[...]).astype(jnp.bfloat16)

    @pl.when(i == 0)
    def _():
        gate_ref[0:FFN_HALO, :] = jnp.zeros((FFN_HALO, FFN_DIM), jnp.float32)

    def up_proj(j):
        gate_j = _nn_dot(h, wup_ref[:, j * FFN_BLOCK:(j + 1) * FFN_BLOCK])
        val_j = _nn_dot(h, wup_ref[:, FFN_DIM + j * FFN_BLOCK:FFN_DIM + (j + 1) * FFN_BLOCK])
        return gate_j, val_j

    n_blocks = FFN_DIM // FFN_BLOCK
    x2 = x1
    up_next = up_proj(0)
    for j in range(n_blocks):
        c0 = j * FFN_BLOCK
        gate_j, val_j = up_next
        if j + 1 < n_blocks:
            up_next = up_proj(j + 1)
        gate_ref[FFN_HALO:FFN_HALO + rows, c0:c0 + FFN_BLOCK] = gate_j
        conv = bf_ref[:, c0:c0 + FFN_BLOCK]
        for tap in range(FFN_CONV_KERNEL):
            off = FFN_HALO - (FFN_CONV_KERNEL - 1) + tap
            conv = conv + wf_ref[tap:tap + 1, c0:c0 + FFN_BLOCK] * gate_ref[off:off + rows, c0:c0 + FFN_BLOCK]
        carry_rows = gate_ref[rows:rows + FFN_HALO, c0:c0 + FFN_BLOCK]
        hidden = (conv * _sigmoid(conv) * val_j).astype(jnp.bfloat16)
        x2 = x2 + _nn_dot(hidden, wdn_ref[c0:c0 + FFN_BLOCK, :])
        gate_ref[0:FFN_HALO, c0:c0 + FFN_BLOCK] = carry_rows
    if final_norm:
        x2 = _rms(x2, gfin_ref[...])
    y_ref[0] = x2


def _ffn_out(x, a, o, w_out, g_ffn, w_up, w_fconv, b_fconv, w_down, g_final, final_norm):
    b, s, d = x.shape
    t = FFN_ROWS
    const = lambda bi, i: (0, 0)
    resident = functools.partial(pl.BlockSpec, index_map=const, pipeline_mode=pl.Buffered(1))
    return pl.pallas_call(
        functools.partial(_ffn_out_kernel, final_norm=final_norm),
        grid=(b, s // t),
        in_specs=[
            pl.BlockSpec((1, t, d), lambda bi, i: (bi, i, 0)),
            pl.BlockSpec((1, t, CONV_WIDTH), lambda bi, i: (bi, i, 0)),
            pl.BlockSpec((1, N_HEADS, t, V_DIM), lambda bi, i: (bi, 0, i, 0)),
            resident((d, d)),
            pl.BlockSpec((1, d), const),
            resident((d, 2 * FFN_DIM)),
            pl.BlockSpec((SUBLANES, FFN_DIM), const),
            pl.BlockSpec((1, FFN_DIM), const),
            resident((FFN_DIM, d)),
            pl.BlockSpec((1, d), const),
        ],
        out_specs=pl.BlockSpec((1, t, d), lambda bi, i: (bi, i, 0)),
        out_shape=jax.ShapeDtypeStruct((b, s, d), jnp.float32),
        scratch_shapes=[pltpu.VMEM((FFN_HALO + t, FFN_DIM), jnp.float32)],
        compiler_params=pltpu.CompilerParams(
            dimension_semantics=("arbitrary", "arbitrary"), vmem_limit_bytes=VMEM_LIMIT),
        name="ffn_out",
    )(x, a, o, w_out, g_ffn, w_up, w_fconv, b_fconv, w_down, g_final)


def _rope_tables(seq):
    inv_freq = 1.0 / (ROPE_THETA ** (jnp.arange(0, HEAD_DIM, 2, dtype=jnp.float32) / HEAD_DIM))
    ang = jnp.arange(seq, dtype=jnp.float32)[:, None] * inv_freq[None, :]
    cos, sin = jnp.cos(ang), jnp.sin(ang)
    reps = LANES // HEAD_DIM
    cos_t = jnp.tile(jnp.concatenate([cos, cos], axis=-1), (1, reps))
    sin_t = jnp.tile(jnp.concatenate([-sin, sin], axis=-1), (1, reps))
    return cos_t, sin_t


def _pad_rows(w, rows):
    return jnp.pad(w, ((0, rows - w.shape[0]), (0, 0)))


def kernel(x, g_mix, w_in, w_dw, b_dw, ln_g, ln_b, lambda_q1, lambda_k1, lambda_q2, lambda_k2,
           g_subln, w_out, g_ffn, w_up, w_fconv, b_fconv, w_down, g_final):
    depth = w_in.shape[0]
    seq = x.shape[1]
    cos_t, sin_t = _rope_tables(seq)
    bf = jnp.bfloat16
    row = lambda v: v.reshape(1, -1).astype(jnp.float32)
    for layer in range(depth):
        w_main = w_in[layer][:, :QK_COLS].astype(bf)
        w_vt = w_in[layer][:, QK_COLS:].T.astype(bf)
        a_pre, q, k, vt = _mix_in(x, row(g_mix[layer]), w_main, w_vt, cos_t, sin_t)
        w_taps = jnp.broadcast_to(w_dw[layer].astype(jnp.float32)[:, None, :],
                                  (CONV_KERNEL, SUBLANES, CONV_WIDTH))
        a = _conv_mod(a_pre, w_taps, row(b_dw[layer]), row(ln_g[layer]), row(ln_b[layer]))
        lam_init = 0.8 - 0.6 * math.exp(-0.3 * layer)
        lam_rows = jnp.stack([lambda_q1[layer], lambda_k1[layer], lambda_q2[layer], lambda_k2[layer]])
        lamv = jnp.pad(lam_rows.astype(jnp.float32),
                       ((0, SUBLANES - 4), (0, LANES - lam_rows.shape[1])))
        o = _diff_attn(q, k, vt, lamv, row(g_subln[layer]), lam_init)
        x = _ffn_out(x, a, o, w_out[layer].astype(bf), row(g_ffn[layer]), w_up[layer].astype(bf),
                     _pad_rows(w_fconv[layer], SUBLANES), row(b_fconv[layer]),
                     w_down[layer].astype(bf), row(g_final), final_norm=(layer == depth - 1))
    return x
```

```python
import functools
import math

import jax
import jax.numpy as jnp
from jax import lax
from jax.experimental import pallas as pl
from jax.experimental.pallas import tpu as pltpu

D_MODEL = 1024
CHUNK = 64
CONV_WIDTH = 512
ATTN_WIDTH = 512
N_HEADS = 4
HEAD_DIM = 64
V_DIM = 2 * HEAD_DIM
CONV_KERNEL = 31
FFN_DIM = 2816
FFN_CONV_KERNEL = 3
ROPE_THETA = 10000.0
RMS_EPS = 1e-6
LN_EPS = 1e-5
SUBLN_EPS = 1e-5
QK_COLS = 2 * CONV_WIDTH + 2 * ATTN_WIDTH

LANES = 128
SUBLANES = 8
LOG2E = math.log2(math.e)
NEG_BIG = -1e30

MIX_ROWS = 512
CONV_ROWS = 512
CONV_HALO = 32
CONV_CHUNK = 32
ATT_Q = 512
ATT_K = 512
ATT_UNROLL = 4
ATT_ONES = 16
ATT_SAFE = 2.0 ** 60
FFN_ROWS = 512
FFN_SPLIT = 2
FFN_HALO = SUBLANES
FFN_BLOCK = 256

VMEM_LIMIT = 56 * 1024 * 1024


def _nt_dot(a, b):
    return lax.dot_general(a, b, (((1,), (1,)), ((), ())), preferred_element_type=jnp.float32)


def _nn_dot(a, b):
    return jnp.dot(a, b, preferred_element_type=jnp.float32)


def _sigmoid(x):
    return 1.0 / (1.0 + jnp.exp(-x))


def _mix_in_kernel(x_ref, g_ref, w_ref, wvt_ref, cos_ref, sin_ref,
                   apre_ref, q_ref, k_ref, vt_ref, *, q_scale):
    x = x_ref[0]
    ms = jnp.mean(x * x, axis=-1, keepdims=True)
    h = (x * lax.rsqrt(ms + RMS_EPS) * g_ref[...]).astype(jnp.bfloat16)

    main = _nn_dot(h, w_ref[...])
    c_val = main[:, :CONV_WIDTH]
    c_gate = main[:, CONV_WIDTH:2 * CONV_WIDTH]
    apre_ref[0] = c_val * _sigmoid(c_gate)

    cos = cos_ref[...]
    sin = sin_ref[...]
    lane = lax.broadcasted_iota(jnp.int32, cos.shape, 1)
    first_half = (lane % HEAD_DIM) < (HEAD_DIM // 2)

    def rope(t):
        up = pltpu.roll(t, LANES - HEAD_DIM // 2, axis=1)
        down = pltpu.roll(t, HEAD_DIM // 2, axis=1)
        return t * cos + jnp.where(first_half, up, down) * sin

    q0 = 2 * CONV_WIDTH
    k0 = q0 + ATTN_WIDTH
    for hd in range(N_HEADS):
        qh = main[:, q0 + hd * V_DIM:q0 + (hd + 1) * V_DIM]
        kh = main[:, k0 + hd * V_DIM:k0 + (hd + 1) * V_DIM]
        q_ref[0, hd] = (rope(qh) * q_scale).astype(jnp.bfloat16)
        k_ref[0, hd] = rope(kh).astype(jnp.bfloat16)

    vt = _nt_dot(wvt_ref[...], h).astype(jnp.bfloat16)
    rows = x.shape[0]
    ones = jnp.ones((ATT_ONES, ATT_K), jnp.bfloat16)
    for j in range(rows // ATT_K):
        for hd in range(N_HEADS):
            vt_ref[0, j, hd, 0:V_DIM, :] = vt[hd * V_DIM:(hd + 1) * V_DIM, j * ATT_K:(j + 1) * ATT_K]
            vt_ref[0, j, hd, V_DIM:V_DIM + ATT_ONES, :] = ones


def _mix_in(x, g_mix, w_main, w_vt, cos_t, sin_t):
    b, s, d = x.shape
    t = MIX_ROWS
    q_scale = HEAD_DIM ** -0.5 * LOG2E
    const = lambda bi, i: (0, 0)
    return pl.pallas_call(
        functools.partial(_mix_in_kernel, q_scale=q_scale),
        grid=(b, s // t),
        in_specs=[
            pl.BlockSpec((1, t, d), lambda bi, i: (bi, i, 0)),
            pl.BlockSpec((1, d), const),
            pl.BlockSpec((d, QK_COLS), const),
            pl.BlockSpec((ATTN_WIDTH, d), const),
            pl.BlockSpec((t, LANES), lambda bi, i: (i, 0)),
            pl.BlockSpec((t, LANES), lambda bi, i: (i, 0)),
        ],
        out_specs=[
            pl.BlockSpec((1, t, CONV_WIDTH), lambda bi, i: (bi, i, 0)),
            pl.BlockSpec((1, N_HEADS, t, V_DIM), lambda bi, i: (bi, 0, i, 0)),
            pl.BlockSpec((1, N_HEADS, t, V_DIM), lambda bi, i: (bi, 0, i, 0)),
            pl.BlockSpec((1, t // ATT_K, N_HEADS, V_DIM + ATT_ONES, ATT_K),
                         lambda bi, i: (bi, i, 0, 0, 0)),
        ],
        out_shape=[
            jax.ShapeDtypeStruct((b, s, CONV_WIDTH), jnp.float32),
            jax.ShapeDtypeStruct((b, N_HEADS, s, V_DIM), jnp.bfloat16),
            jax.ShapeDtypeStruct((b, N_HEADS, s, V_DIM), jnp.bfloat16),
            jax.ShapeDtypeStruct((b, s // ATT_K, N_HEADS, V_DIM + ATT_ONES, ATT_K), jnp.bfloat16),
        ],
        compiler_params=pltpu.CompilerParams(
            dimension_semantics=("arbitrary", "arbitrary"), vmem_limit_bytes=VMEM_LIMIT),
        name="mix_in",
    )(x, g_mix, w_main, w_vt, cos_t, sin_t)


def _conv_mod_kernel(cur_ref, halo_ref, w_ref, b_ref, lng_ref, lnb_ref, a_ref, u_ref):
    i = pl.program_id(1)
    rows = cur_ref.shape[1]
    halo = halo_ref[0]
    u_ref[0, 0:CONV_HALO, :] = jnp.where(i == 0, jnp.zeros_like(halo), halo)
    u_ref[0, CONV_HALO:CONV_HALO + rows, :] = cur_ref[0]
    span = CONV_HALO + rows - SUBLANES
    for r in range(1, SUBLANES):
        u_ref[r, 0:span, :] = u_ref[0, r:r + span, :]

    bias = b_ref[...]
    lng = lng_ref[...]
    lnb = lnb_ref[...]
    first_tap = CONV_HALO - (CONV_KERNEL - 1)

    groups = CONV_CHUNK // SUBLANES

    def chunk(c, carry):
        base = c * CONV_CHUNK
        accs = [jnp.broadcast_to(bias, (CONV_CHUNK, CONV_WIDTH)), None]
        for tap in range(CONV_KERNEL):
            off = first_tap + tap
            aligned = base + (off // SUBLANES) * SUBLANES
            w_tap = jnp.concatenate([w_ref[tap]] * groups, axis=0)
            term = w_tap * u_ref[off % SUBLANES, pl.ds(aligned, CONV_CHUNK), :]
            accs[tap % 2] = term if accs[tap % 2] is None else accs[tap % 2] + term
        acc = accs[0] + accs[1]
        mu = jnp.mean(acc, axis=-1, keepdims=True)
        cen = acc - mu
        var = jnp.mean(cen * cen, axis=-1, keepdims=True)
        y = cen * lax.rsqrt(var + LN_EPS) * lng + lnb
        a_ref[0, pl.ds(base, CONV_CHUNK), :] = (y * _sigmoid(y)).astype(a_ref.dtype)
        return carry

    for c in range(rows // CONV_CHUNK):
        chunk(c, 0)


def _conv_mod(a_pre, w_dw, b_dw, ln_g, ln_b):
    b, s, c = a_pre.shape
    t = CONV_ROWS
    per = t // CONV_HALO
    const = lambda bi, i: (0, 0)
    return pl.pallas_call(
        _conv_mod_kernel,
        grid=(b, s // t),
        in_specs=[
            pl.BlockSpec((1, t, c), lambda bi, i: (bi, i, 0)),
            pl.BlockSpec((1, CONV_HALO, c), lambda bi, i: (bi, jnp.maximum(i * per - 1, 0), 0)),
            pl.BlockSpec((CONV_KERNEL, SUBLANES, c), lambda bi, i: (0, 0, 0)),
            pl.BlockSpec((1, c), const),
            pl.BlockSpec((1, c), const),
            pl.BlockSpec((1, c), const),
        ],
        out_specs=pl.BlockSpec((1, t, c), lambda bi, i: (bi, i, 0)),
        out_shape=jax.ShapeDtypeStruct((b, s, c), jnp.bfloat16),
        scratch_shapes=[pltpu.VMEM((SUBLANES, CONV_HALO + t, c), jnp.float32)],
        compiler_params=pltpu.CompilerParams(
            dimension_semantics=("arbitrary", "arbitrary"), vmem_limit_bytes=VMEM_LIMIT),
        name="conv_mod",
    )(a_pre, a_pre, w_dw, b_dw, ln_g, ln_b)


def _diff_attn_kernel(q_ref, k_ref, vt_ref, lamv_ref, g_ref, o_ref,
                      m_ref, acc_ref, qq_ref, sa_ref, sb_ref, mask_ref, *, lam_init):
    seq = q_ref.shape[2]
    tq, tk = ATT_Q, ATT_K
    lamv = lamv_ref[...]
    lam = (jnp.exp(jnp.sum(lamv[0:1] * lamv[1:2], axis=-1, keepdims=True))
           - jnp.exp(jnp.sum(lamv[2:3] * lamv[3:4], axis=-1, keepdims=True)) + lam_init)
    gain = g_ref[...] * (1.0 - lam_init)

    lane = lax.broadcasted_iota(jnp.int32, (tq, V_DIM), 1)
    key_chunk = lax.broadcasted_iota(jnp.int32, (tk, 2 * tq), 0) // CHUNK
    qry_chunk = (lax.broadcasted_iota(jnp.int32, (tk, 2 * tq), 1) % tq) // CHUNK
    mask_ref[...] = jnp.where(key_chunk <= qry_chunk, 0.0, NEG_BIG)

    def scores(ki, s_ref):
        k0 = pl.multiple_of(ki * tk, tk)
        s_ref[...] = _nt_dot(k_ref[0, 0, pl.ds(k0, tk), :], qq_ref[...])

    def values(ki, p):
        return _nn_dot(vt_ref[0, ki, 0], p.astype(jnp.bfloat16))

    def anchor(qi, s_ref):
        m_ref[...] = jnp.max(s_ref[0:CHUNK, :], axis=0, keepdims=True)
        acc_ref[...] = values(qi, jnp.exp2(s_ref[...] + mask_ref[...] - m_ref[...]))

    def add_tile(ki, s_ref):
        acc_ref[...] += values(ki, jnp.exp2(s_ref[...] - m_ref[...]))

    def online_tile(ki, s_ref, masked):
        s = s_ref[...]
        if masked:
            s = s + mask_ref[...]
        m_old = m_ref[...]
        m_new = jnp.maximum(m_old, jnp.max(s, axis=0, keepdims=True))
        acc_ref[...] = jnp.exp2(m_old - m_new) * acc_ref[...] + values(ki, jnp.exp2(s - m_new))
        m_ref[...] = m_new

    def settle(tile):
        denom = acc_ref[V_DIM:V_DIM + 1, :]
        in_range = jnp.logical_and(denom < ATT_SAFE, denom > 1.0 / ATT_SAFE)
        overflowed = jnp.max(jnp.where(in_range, 0.0, 1.0)) > 0.5

        @pl.when(overflowed)
        def _():
            m_ref[...] = jnp.full(m_ref.shape, NEG_BIG, jnp.float32)
            acc_ref[...] = jnp.zeros(acc_ref.shape, jnp.float32)

            def step(ki, c):
                scores(ki, sa_ref)
                online_tile(ki, sa_ref, False)
                return c

            lax.fori_loop(0, tile, step, 0)
            scores(tile, sa_ref)
            online_tile(tile, sa_ref, True)

    def emit(tile):
        acc = acc_ref[...]
        inv_l = 1.0 / acc[V_DIM:V_DIM + 1, :]
        o_t = (acc[:V_DIM, :tq] * inv_l[:, :tq]
               - lam * (acc[:V_DIM, tq:] * inv_l[:, tq:]))
        o = o_t.T
        ms = jnp.mean(o * o, axis=-1, keepdims=True)
        r0 = pl.multiple_of(tile * tq, tq)
        o_ref[0, 0, pl.ds(r0, tq), :] = (o * lax.rsqrt(ms + SUBLN_EPS) * gain).astype(o_ref.dtype)

    def q_tile(qi, carry):
        prev = jnp.maximum(qi - 1, 0)
        settle(prev)
        emit(prev)

        q0 = pl.multiple_of(qi * tq, tq)
        qt = q_ref[0, 0, pl.ds(q0, tq), :]
        zero = jnp.zeros_like(qt)
        qq_ref[0:tq, :] = jnp.where(lane < HEAD_DIM, qt, zero)
        qq_ref[tq:2 * tq, :] = jnp.where(lane >= HEAD_DIM, qt, zero)

        scores(qi, sb_ref)
        scores(0, sa_ref)
        anchor(qi, sb_ref)

        bufs = (sa_ref, sb_ref)

        def run(first, count, lookahead):
            for j in range(count):
                if j + 1 < count or lookahead:
                    scores(first + j + 1, bufs[(j + 1) % 2])
                add_tile(first + j, bufs[j % 2])

        n_groups = (qi - 1) // ATT_UNROLL

        def group(i, c):
            run(ATT_UNROLL * i, ATT_UNROLL, True)
            return c

        lax.fori_loop(0, n_groups, group, 0)

        left = qi - ATT_UNROLL * jnp.maximum(n_groups, 0)
        for r in range(1, ATT_UNROLL + 1):
            @pl.when(left == r)
            def _(r=r):
                run(qi - r, r, False)

        return carry

    n_q = seq // tq
    acc_ref[...] = jnp.ones(acc_ref.shape, jnp.float32)
    lax.fori_loop(0, n_q, q_tile, 0)
    settle(n_q - 1)
    emit(n_q - 1)


def _diff_attn(q, k, vt, lamv, g_subln, lam_init):
    b, nh, s, dv = q.shape
    return pl.pallas_call(
        functools.partial(_diff_attn_kernel, lam_init=lam_init),
        grid=(b, nh),
        in_specs=[
            pl.BlockSpec((1, 1, s, dv), lambda bi, hi: (bi, hi, 0, 0)),
            pl.BlockSpec((1, 1, s, dv), lambda bi, hi: (bi, hi, 0, 0)),
            pl.BlockSpec((1, s // ATT_K, 1, dv + ATT_ONES, ATT_K), lambda bi, hi: (bi, 0, hi, 0, 0)),
            pl.BlockSpec((SUBLANES, LANES), lambda bi, hi: (0, 0)),
            pl.BlockSpec((1, dv), lambda bi, hi: (0, 0)),
        ],
        out_specs=pl.BlockSpec((1, 1, s, dv), lambda bi, hi: (bi, hi, 0, 0)),
        out_shape=jax.ShapeDtypeStruct((b, nh, s, dv), jnp.bfloat16),
        scratch_shapes=[
            pltpu.VMEM((1, 2 * ATT_Q), jnp.float32),
            pltpu.VMEM((dv + ATT_ONES, 2 * ATT_Q), jnp.float32),
            pltpu.VMEM((2 * ATT_Q, dv), jnp.bfloat16),
            pltpu.VMEM((ATT_K, 2 * ATT_Q), jnp.float32),
            pltpu.VMEM((ATT_K, 2 * ATT_Q), jnp.float32),
            pltpu.VMEM((ATT_K, 2 * ATT_Q), jnp.float32),
        ],
        compiler_params=pltpu.CompilerParams(
            dimension_semantics=("arbitrary", "arbitrary"), vmem_limit_bytes=VMEM_LIMIT),
        name="diff_attn",
    )(q, k, vt, lamv, g_subln)


def _rms(x, g):
    ms = jnp.mean(x * x, axis=-1, keepdims=True)
    return x * lax.rsqrt(ms + RMS_EPS) * g


def _ffn_out_kernel(x_ref, a_ref, o_ref, wout_ref, gffn_ref, wup_ref, wf_ref, bf_ref, wdn_ref,
                    gfin_ref, y_ref, gate_ref, *, final_norm):
    i = pl.program_id(1)
    rows = x_ref.shape[1]
    sub = rows // FFN_SPLIT
    parts = range(FFN_SPLIT)
    n_blocks = FFN_DIM // FFN_BLOCK

    x1 = []
    for s in parts:
        r = slice(s * sub, (s + 1) * sub)
        o_cat = jnp.concatenate([o_ref[0, hd, r, :] for hd in range(N_HEADS)], axis=-1)
        ao = jnp.concatenate([a_ref[0, r, :], o_cat], axis=-1)
        x1.append(x_ref[0, r, :] + _nn_dot(ao, wout_ref[...]))
    h = [_rms(x1[s], gffn_ref[...]).astype(jnp.bfloat16) for s in parts]

    @pl.when(i == 0)
    def _():
        gate_ref[0:FFN_HALO, :] = jnp.zeros((FFN_HALO, FFN_DIM), jnp.float32)

    def up_proj(s, j):
        gate_j = _nn_dot(h[s], wup_ref[:, j * FFN_BLOCK:(j + 1) * FFN_BLOCK])
        val_j = _nn_dot(h[s], wup_ref[:, FFN_DIM + j * FFN_BLOCK:FFN_DIM + (j + 1) * FFN_BLOCK])
        return gate_j, val_j

    x2 = list(x1)
    up_next = [up_proj(s, 0) for s in parts]
    for j in range(n_blocks):
        c0 = j * FFN_BLOCK
        cols = slice(c0, c0 + FFN_BLOCK)
        for s in parts:
            gate_j, val_j = up_next[s]
            if j + 1 < n_blocks:
                up_next[s] = up_proj(s, j + 1)
            base = FFN_HALO + s * sub
            gate_ref[base:base + sub, cols] = gate_j
            conv = bf_ref[:, cols]
            for tap in range(FFN_CONV_KERNEL):
                off = base - (FFN_CONV_KERNEL - 1) + tap
                conv = conv + wf_ref[tap:tap + 1, cols] * gate_ref[off:off + sub, cols]
            hidden = (conv * _sigmoid(conv) * val_j).astype(jnp.bfloat16)
            x2[s] = x2[s] + _nn_dot(hidden, wdn_ref[cols, :])
        carry_rows = gate_ref[rows:rows + FFN_HALO, cols]
        gate_ref[0:FFN_HALO, cols] = carry_rows
    for s in parts:
        out = _rms(x2[s], gfin_ref[...]) if final_norm else x2[s]
        y_ref[0, s * sub:(s + 1) * sub, :] = out


def _ffn_out(x, a, o, w_out, g_ffn, w_up, w_fconv, b_fconv, w_down, g_final, final_norm):
    b, s, d = x.shape
    t = FFN_ROWS
    const = lambda bi, i: (0, 0)
    resident = functools.partial(pl.BlockSpec, index_map=const, pipeline_mode=pl.Buffered(1))
    return pl.pallas_call(
        functools.partial(_ffn_out_kernel, final_norm=final_norm),
        grid=(b, s // t),
        in_specs=[
            pl.BlockSpec((1, t, d), lambda bi, i: (bi, i, 0)),
            pl.BlockSpec((1, t, CONV_WIDTH), lambda bi, i: (bi, i, 0)),
            pl.BlockSpec((1, N_HEADS, t, V_DIM), lambda bi, i: (bi, 0, i, 0)),
            resident((d, d)),
            pl.BlockSpec((1, d), const),
            resident((d, 2 * FFN_DIM)),
            pl.BlockSpec((SUBLANES, FFN_DIM), const),
            pl.BlockSpec((1, FFN_DIM), const),
            resident((FFN_DIM, d)),
            pl.BlockSpec((1, d), const),
        ],
        out_specs=pl.BlockSpec((1, t, d), lambda bi, i: (bi, i, 0)),
        out_shape=jax.ShapeDtypeStruct((b, s, d), jnp.float32),
        scratch_shapes=[pltpu.VMEM((FFN_HALO + t, FFN_DIM), jnp.float32)],
        compiler_params=pltpu.CompilerParams(
            dimension_semantics=("arbitrary", "arbitrary"), vmem_limit_bytes=VMEM_LIMIT),
        name="ffn_out",
    )(x, a, o, w_out, g_ffn, w_up, w_fconv, b_fconv, w_down, g_final)


def _rope_tables(seq):
    inv_freq = 1.0 / (ROPE_THETA ** (jnp.arange(0, HEAD_DIM, 2, dtype=jnp.float32) / HEAD_DIM))
    ang = jnp.arange(seq, dtype=jnp.float32)[:, None] * inv_freq[None, :]
    cos, sin = jnp.cos(ang), jnp.sin(ang)
    reps = LANES // HEAD_DIM
    cos_t = jnp.tile(jnp.concatenate([cos, cos], axis=-1), (1, reps))
    sin_t = jnp.tile(jnp.concatenate([-sin, sin], axis=-1), (1, reps))
    return cos_t, sin_t


def _pad_rows(w, rows):
    return jnp.pad(w, ((0, rows - w.shape[0]), (0, 0)))


def kernel(x, g_mix, w_in, w_dw, b_dw, ln_g, ln_b, lambda_q1, lambda_k1, lambda_q2, lambda_k2,
           g_subln, w_out, g_ffn, w_up, w_fconv, b_fconv, w_down, g_final):
    depth = w_in.shape[0]
    seq = x.shape[1]
    cos_t, sin_t = _rope_tables(seq)
    bf = jnp.bfloat16
    row = lambda v: v.reshape(1, -1).astype(jnp.float32)
    for layer in range(depth):
        w_main = w_in[layer][:, :QK_COLS].astype(bf)
        w_vt = w_in[layer][:, QK_COLS:].T.astype(bf)
        a_pre, q, k, vt = _mix_in(x, row(g_mix[layer]), w_main, w_vt, cos_t, sin_t)
        w_taps = jnp.broadcast_to(w_dw[layer].astype(jnp.float32)[:, None, :],
                                  (CONV_KERNEL, SUBLANES, CONV_WIDTH))
        a = _conv_mod(a_pre, w_taps, row(b_dw[layer]), row(ln_g[layer]), row(ln_b[layer]))
        lam_init = 0.8 - 0.6 * math.exp(-0.3 * layer)
        lam_rows = jnp.stack([lambda_q1[layer], lambda_k1[layer], lambda_q2[layer], lambda_k2[layer]])
        lamv = jnp.pad(lam_rows.astype(jnp.float32),
                       ((0, SUBLANES - 4), (0, LANES - lam_rows.shape[1])))
        o = _diff_attn(q, k, vt, lamv, row(g_subln[layer]), lam_init)
        x = _ffn_out(x, a, o, w_out[layer].astype(bf), row(g_ffn[layer]), w_up[layer].astype(bf),
                     _pad_rows(w_fconv[layer], SUBLANES), row(b_fconv[layer]),
                     w_down[layer].astype(bf), row(g_final), final_norm=(layer == depth - 1))
    return x
```

```python
import functools
import math

import jax
import jax.numpy as jnp
from jax import lax
from jax.experimental import pallas as pl
from jax.experimental.pallas import tpu as pltpu

D_MODEL = 1024
CHUNK = 64
CONV_WIDTH = 512
ATTN_WIDTH = 512
N_HEADS = 4
HEAD_DIM = 64
V_DIM = 2 * HEAD_DIM
CONV_KERNEL = 31
FFN_DIM = 2816
FFN_CONV_KERNEL = 3
ROPE_THETA = 10000.0
RMS_EPS = 1e-6
LN_EPS = 1e-5
SUBLN_EPS = 1e-5
QK_COLS = 2 * CONV_WIDTH + 2 * ATTN_WIDTH

LANES = 128
SUBLANES = 8
LOG2E = math.log2(math.e)
NEG_BIG = -1e30

MIX_ROWS = 512
MIX_SPLIT = 2
CONV_ROWS = 512
CONV_HALO = 32
CONV_CHUNK = 32
ATT_Q = 512
ATT_K = 512
ATT_UNROLL = 4
ATT_ONES = 16
ATT_SAFE = 2.0 ** 60
FFN_ROWS = 512
FFN_SPLIT = 2
FFN_HALO = SUBLANES
FFN_BLOCK = 256

VMEM_LIMIT = 56 * 1024 * 1024


def _nt_dot(a, b):
    return lax.dot_general(a, b, (((1,), (1,)), ((), ())), preferred_element_type=jnp.float32)


def _nn_dot(a, b):
    return jnp.dot(a, b, preferred_element_type=jnp.float32)


def _sigmoid(x):
    return 1.0 / (1.0 + jnp.exp(-x))


def _mix_in_kernel(x_ref, g_ref, w_ref, wvt_ref, cos_ref, sin_ref,
                   apre_ref, q_ref, k_ref, vt_ref, *, q_scale):
    rows = x_ref.shape[1]
    sub = rows // MIX_SPLIT
    lane = lax.broadcasted_iota(jnp.int32, (sub, LANES), 1)
    first_half = (lane % HEAD_DIM) < (HEAD_DIM // 2)
    q0 = 2 * CONV_WIDTH
    k0 = q0 + ATTN_WIDTH
    ones = jnp.ones((ATT_ONES, sub), jnp.bfloat16)

    h = []
    for s in range(MIX_SPLIT):
        x = x_ref[0, s * sub:(s + 1) * sub, :]
        ms = jnp.mean(x * x, axis=-1, keepdims=True)
        h.append((x * lax.rsqrt(ms + RMS_EPS) * g_ref[...]).astype(jnp.bfloat16))
    main = [_nn_dot(h[s], w_ref[...]) for s in range(MIX_SPLIT)]

    for s in range(MIX_SPLIT):
        r = slice(s * sub, (s + 1) * sub)
        c_val = main[s][:, :CONV_WIDTH]
        c_gate = main[s][:, CONV_WIDTH:2 * CONV_WIDTH]
        apre_ref[0, r, :] = c_val * _sigmoid(c_gate)

        cos = cos_ref[r, :]
        sin = sin_ref[r, :]

        def rope(t):
            up = pltpu.roll(t, LANES - HEAD_DIM // 2, axis=1)
            down = pltpu.roll(t, HEAD_DIM // 2, axis=1)
            return t * cos + jnp.where(first_half, up, down) * sin

        for hd in range(N_HEADS):
            qh = main[s][:, q0 + hd * V_DIM:q0 + (hd + 1) * V_DIM]
            kh = main[s][:, k0 + hd * V_DIM:k0 + (hd + 1) * V_DIM]
            q_ref[0, hd, r, :] = (rope(qh) * q_scale).astype(jnp.bfloat16)
            k_ref[0, hd, r, :] = rope(kh).astype(jnp.bfloat16)

    for s in range(MIX_SPLIT):
        vt = _nt_dot(wvt_ref[...], h[s]).astype(jnp.bfloat16)
        j, c0 = divmod(s * sub, ATT_K)
        for hd in range(N_HEADS):
            vt_ref[0, j, hd, 0:V_DIM, c0:c0 + sub] = vt[hd * V_DIM:(hd + 1) * V_DIM, :]
            vt_ref[0, j, hd, V_DIM:V_DIM + ATT_ONES, c0:c0 + sub] = ones


def _mix_in(x, g_mix, w_main, w_vt, cos_t, sin_t):
    b, s, d = x.shape
    t = MIX_ROWS
    q_scale = HEAD_DIM ** -0.5 * LOG2E
    const = lambda bi, i: (0, 0)
    return pl.pallas_call(
        functools.partial(_mix_in_kernel, q_scale=q_scale),
        grid=(b, s // t),
        in_specs=[
            pl.BlockSpec((1, t, d), lambda bi, i: (bi, i, 0)),
            pl.BlockSpec((1, d), const),
            pl.BlockSpec((d, QK_COLS), const),
            pl.BlockSpec((ATTN_WIDTH, d), const),
            pl.BlockSpec((t, LANES), lambda bi, i: (i, 0)),
            pl.BlockSpec((t, LANES), lambda bi, i: (i, 0)),
        ],
        out_specs=[
            pl.BlockSpec((1, t, CONV_WIDTH), lambda bi, i: (bi, i, 0)),
            pl.BlockSpec((1, N_HEADS, t, V_DIM), lambda bi, i: (bi, 0, i, 0)),
            pl.BlockSpec((1, N_HEADS, t, V_DIM), lambda bi, i: (bi, 0, i, 0)),
            pl.BlockSpec((1, t // ATT_K, N_HEADS, V_DIM + ATT_ONES, ATT_K),
                         lambda bi, i: (bi, i, 0, 0, 0)),
        ],
        out_shape=[
            jax.ShapeDtypeStruct((b, s, CONV_WIDTH), jnp.float32),
            jax.ShapeDtypeStruct((b, N_HEADS, s, V_DIM), jnp.bfloat16),
            jax.ShapeDtypeStruct((b, N_HEADS, s, V_DIM), jnp.bfloat16),
            jax.ShapeDtypeStruct((b, s // ATT_K, N_HEADS, V_DIM + ATT_ONES, ATT_K), jnp.bfloat16),
        ],
        compiler_params=pltpu.CompilerParams(
            dimension_semantics=("arbitrary", "arbitrary"), vmem_limit_bytes=VMEM_LIMIT),
        name="mix_in",
    )(x, g_mix, w_main, w_vt, cos_t, sin_t)


def _conv_mod_kernel(cur_ref, halo_ref, w_ref, b_ref, lng_ref, lnb_ref, a_ref, u_ref):
    i = pl.program_id(1)
    rows = cur_ref.shape[1]
    halo = halo_ref[0]
    u_ref[0, 0:CONV_HALO, :] = jnp.where(i == 0, jnp.zeros_like(halo), halo)
    u_ref[0, CONV_HALO:CONV_HALO + rows, :] = cur_ref[0]
    span = CONV_HALO + rows - SUBLANES
    for r in range(1, SUBLANES):
        u_ref[r, 0:span, :] = u_ref[0, r:r + span, :]

    bias = b_ref[...]
    lng = lng_ref[...]
    lnb = lnb_ref[...]
    first_tap = CONV_HALO - (CONV_KERNEL - 1)

    groups = CONV_CHUNK // SUBLANES

    def chunk(c, carry):
        base = c * CONV_CHUNK
        accs = [jnp.broadcast_to(bias, (CONV_CHUNK, CONV_WIDTH)), None]
        for tap in range(CONV_KERNEL):
            off = first_tap + tap
            aligned = base + (off // SUBLANES) * SUBLANES
            w_tap = jnp.concatenate([w_ref[tap]] * groups, axis=0)
            term = w_tap * u_ref[off % SUBLANES, pl.ds(aligned, CONV_CHUNK), :]
            accs[tap % 2] = term if accs[tap % 2] is None else accs[tap % 2] + term
        acc = accs[0] + accs[1]
        mu = jnp.mean(acc, axis=-1, keepdims=True)
        cen = acc - mu
        var = jnp.mean(cen * cen, axis=-1, keepdims=True)
        y = cen * lax.rsqrt(var + LN_EPS) * lng + lnb
        a_ref[0, pl.ds(base, CONV_CHUNK), :] = (y * _sigmoid(y)).astype(a_ref.dtype)
        return carry

    for c in range(rows // CONV_CHUNK):
        chunk(c, 0)


def _conv_mod(a_pre, w_dw, b_dw, ln_g, ln_b):
    b, s, c = a_pre.shape
    t = CONV_ROWS
    per = t // CONV_HALO
    const = lambda bi, i: (0, 0)
    return pl.pallas_call(
        _conv_mod_kernel,
        grid=(b, s // t),
        in_specs=[
            pl.BlockSpec((1, t, c), lambda bi, i: (bi, i, 0)),
            pl.BlockSpec((1, CONV_HALO, c), lambda bi, i: (bi, jnp.maximum(i * per - 1, 0), 0)),
            pl.BlockSpec((CONV_KERNEL, SUBLANES, c), lambda bi, i: (0, 0, 0)),
            pl.BlockSpec((1, c), const),
            pl.BlockSpec((1, c), const),
            pl.BlockSpec((1, c), const),
        ],
        out_specs=pl.BlockSpec((1, t, c), lambda bi, i: (bi, i, 0)),
        out_shape=jax.ShapeDtypeStruct((b, s, c), jnp.bfloat16),
        scratch_shapes=[pltpu.VMEM((SUBLANES, CONV_HALO + t, c), jnp.float32)],
        compiler_params=pltpu.CompilerParams(
            dimension_semantics=("arbitrary", "arbitrary"), vmem_limit_bytes=VMEM_LIMIT),
        name="conv_mod",
    )(a_pre, a_pre, w_dw, b_dw, ln_g, ln_b)


def _diff_attn_kernel(q_ref, k_ref, vt_ref, lamv_ref, g_ref, o_ref,
                      m_ref, acc_ref, qq_ref, sa_ref, sb_ref, mask_ref, *, lam_init):
    seq = q_ref.shape[2]
    tq, tk = ATT_Q, ATT_K
    lamv = lamv_ref[...]
    lam = (jnp.exp(jnp.sum(lamv[0:1] * lamv[1:2], axis=-1, keepdims=True))
           - jnp.exp(jnp.sum(lamv[2:3] * lamv[3:4], axis=-1, keepdims=True)) + lam_init)
    gain = g_ref[...] * (1.0 - lam_init)

    lane = lax.broadcasted_iota(jnp.int32, (tq, V_DIM), 1)
    key_chunk = lax.broadcasted_iota(jnp.int32, (tk, 2 * tq), 0) // CHUNK
    qry_chunk = (lax.broadcasted_iota(jnp.int32, (tk, 2 * tq), 1) % tq) // CHUNK
    @pl.when(jnp.logical_and(pl.program_id(0) == 0, pl.program_id(1) == 0))
    def _():
        mask_ref[...] = jnp.where(key_chunk <= qry_chunk, 0.0, NEG_BIG)

    def scores(ki, s_ref):
        k0 = pl.multiple_of(ki * tk, tk)
        s_ref[...] = _nt_dot(k_ref[0, 0, pl.ds(k0, tk), :], qq_ref[...])

    def values(ki, p):
        return _nn_dot(vt_ref[0, ki, 0], p.astype(jnp.bfloat16))

    def anchor(qi, s_ref):
        m_ref[...] = jnp.max(s_ref[0:CHUNK, :], axis=0, keepdims=True)
        acc_ref[...] = values(qi, jnp.exp2(s_ref[...] + mask_ref[...] - m_ref[...]))

    def add_tile(ki, s_ref):
        acc_ref[...] += values(ki, jnp.exp2(s_ref[...] - m_ref[...]))

    def online_tile(ki, s_ref, masked):
        s = s_ref[...]
        if masked:
            s = s + mask_ref[...]
        m_old = m_ref[...]
        m_new = jnp.maximum(m_old, jnp.max(s, axis=0, keepdims=True))
        acc_ref[...] = jnp.exp2(m_old - m_new) * acc_ref[...] + values(ki, jnp.exp2(s - m_new))
        m_ref[...] = m_new

    def settle(tile):
        denom = acc_ref[V_DIM:V_DIM + 1, :]
        in_range = jnp.logical_and(denom < ATT_SAFE, denom > 1.0 / ATT_SAFE)
        overflowed = jnp.max(jnp.where(in_range, 0.0, 1.0)) > 0.5

        @pl.when(overflowed)
        def _():
            m_ref[...] = jnp.full(m_ref.shape, NEG_BIG, jnp.float32)
            acc_ref[...] = jnp.zeros(acc_ref.shape, jnp.float32)

            def step(ki, c):
                scores(ki, sa_ref)
                online_tile(ki, sa_ref, False)
                return c

            lax.fori_loop(0, tile, step, 0)
            scores(tile, sa_ref)
            online_tile(tile, sa_ref, True)

    def emit(tile):
        acc = acc_ref[...]
        inv_l = 1.0 / acc[V_DIM:V_DIM + 1, :]
        o_t = (acc[:V_DIM, :tq] * inv_l[:, :tq]
               - lam * (acc[:V_DIM, tq:] * inv_l[:, tq:]))
        o = o_t.T
        ms = jnp.mean(o * o, axis=-1, keepdims=True)
        r0 = pl.multiple_of(tile * tq, tq)
        o_ref[0, 0, pl.ds(r0, tq), :] = (o * lax.rsqrt(ms + SUBLN_EPS) * gain).astype(o_ref.dtype)

    def q_tile(qi, carry):
        prev = jnp.maximum(qi - 1, 0)
        settle(prev)
        emit(prev)

        q0 = pl.multiple_of(qi * tq, tq)
        qt = q_ref[0, 0, pl.ds(q0, tq), :]
        zero = jnp.zeros_like(qt)
        qq_ref[0:tq, :] = jnp.where(lane < HEAD_DIM, qt, zero)
        qq_ref[tq:2 * tq, :] = jnp.where(lane >= HEAD_DIM, qt, zero)

        scores(qi, sb_ref)
        scores(0, sa_ref)
        anchor(qi, sb_ref)

        bufs = (sa_ref, sb_ref)

        def run(first, count, lookahead):
            for j in range(count):
                if j + 1 < count or lookahead:
                    scores(first + j + 1, bufs[(j + 1) % 2])
                add_tile(first + j, bufs[j % 2])

        n_groups = (qi - 1) // ATT_UNROLL

        def group(i, c):
            run(ATT_UNROLL * i, ATT_UNROLL, True)
            return c

        lax.fori_loop(0, n_groups, group, 0)

        left = qi - ATT_UNROLL * jnp.maximum(n_groups, 0)
        for r in range(1, ATT_UNROLL + 1):
            @pl.when(left == r)
            def _(r=r):
                run(qi - r, r, False)

        return carry

    n_q = seq // tq
    acc_ref[...] = jnp.ones(acc_ref.shape, jnp.float32)
    lax.fori_loop(0, n_q, q_tile, 0)
    settle(n_q - 1)
    emit(n_q - 1)


def _diff_attn(q, k, vt, lamv, g_subln, lam_init):
    b, nh, s, dv = q.shape
    return pl.pallas_call(
        functools.partial(_diff_attn_kernel, lam_init=lam_init),
        grid=(b, nh),
        in_specs=[
            pl.BlockSpec((1, 1, s, dv), lambda bi, hi: (bi, hi, 0, 0)),
            pl.BlockSpec((1, 1, s, dv), lambda bi, hi: (bi, hi, 0, 0)),
            pl.BlockSpec((1, s // ATT_K, 1, dv + ATT_ONES, ATT_K), lambda bi, hi: (bi, 0, hi, 0, 0)),
            pl.BlockSpec((SUBLANES, LANES), lambda bi, hi: (0, 0)),
            pl.BlockSpec((1, dv), lambda bi, hi: (0, 0)),
        ],
        out_specs=pl.BlockSpec((1, 1, s, dv), lambda bi, hi: (bi, hi, 0, 0)),
        out_shape=jax.ShapeDtypeStruct((b, nh, s, dv), jnp.bfloat16),
        scratch_shapes=[
            pltpu.VMEM((1, 2 * ATT_Q), jnp.float32),
            pltpu.VMEM((dv + ATT_ONES, 2 * ATT_Q), jnp.float32),
            pltpu.VMEM((2 * ATT_Q, dv), jnp.bfloat16),
            pltpu.VMEM((ATT_K, 2 * ATT_Q), jnp.float32),
            pltpu.VMEM((ATT_K, 2 * ATT_Q), jnp.float32),
            pltpu.VMEM((ATT_K, 2 * ATT_Q), jnp.float32),
        ],
        compiler_params=pltpu.CompilerParams(
            dimension_semantics=("arbitrary", "arbitrary"), vmem_limit_bytes=VMEM_LIMIT),
        name="diff_attn",
    )(q, k, vt, lamv, g_subln)


def _rms(x, g):
    ms = jnp.mean(x * x, axis=-1, keepdims=True)
    return x * lax.rsqrt(ms + RMS_EPS) * g


def _ffn_out_kernel(x_ref, a_ref, o_ref, wout_ref, gffn_ref, wup_ref, wf_ref, bf_ref, wdn_ref,
                    gfin_ref, y_ref, gate_ref, *, final_norm):
    i = pl.program_id(1)
    rows = x_ref.shape[1]
    sub = rows // FFN_SPLIT
    parts = range(FFN_SPLIT)
    n_blocks = FFN_DIM // FFN_BLOCK

    x1 = []
    for s in parts:
        r = slice(s * sub, (s + 1) * sub)
        o_cat = jnp.concatenate([o_ref[0, hd, r, :] for hd in range(N_HEADS)], axis=-1)
        ao = jnp.concatenate([a_ref[0, r, :], o_cat], axis=-1)
        x1.append(x_ref[0, r, :] + _nn_dot(ao, wout_ref[...]))
    h = [_rms(x1[s], gffn_ref[...]).astype(jnp.bfloat16) for s in parts]

    @pl.when(i == 0)
    def _():
        gate_ref[0:FFN_HALO, :] = jnp.zeros((FFN_HALO, FFN_DIM), jnp.float32)

    def up_proj(s, j):
        gate_j = _nn_dot(h[s], wup_ref[:, j * FFN_BLOCK:(j + 1) * FFN_BLOCK])
        val_j = _nn_dot(h[s], wup_ref[:, FFN_DIM + j * FFN_BLOCK:FFN_DIM + (j + 1) * FFN_BLOCK])
        return gate_j, val_j

    x2 = list(x1)
    up_next = [up_proj(s, 0) for s in parts]
    for j in range(n_blocks):
        c0 = j * FFN_BLOCK
        cols = slice(c0, c0 + FFN_BLOCK)
        for s in parts:
            gate_j, val_j = up_next[s]
            if j + 1 < n_blocks:
                up_next[s] = up_proj(s, j + 1)
            base = FFN_HALO + s * sub
            gate_ref[base:base + sub, cols] = gate_j
            conv = bf_ref[:, cols]
            for tap in range(FFN_CONV_KERNEL):
                off = base - (FFN_CONV_KERNEL - 1) + tap
                conv = conv + wf_ref[tap:tap + 1, cols] * gate_ref[off:off + sub, cols]
            hidden = (conv * _sigmoid(conv) * val_j).astype(jnp.bfloat16)
            x2[s] = x2[s] + _nn_dot(hidden, wdn_ref[cols, :])
        carry_rows = gate_ref[rows:rows + FFN_HALO, cols]
        gate_ref[0:FFN_HALO, cols] = carry_rows
    for s in parts:
        out = _rms(x2[s], gfin_ref[...]) if final_norm else x2[s]
        y_ref[0, s * sub:(s + 1) * sub, :] = out


def _ffn_out(x, a, o, w_out, g_ffn, w_up, w_fconv, b_fconv, w_down, g_final, final_norm):
    b, s, d = x.shape
    t = FFN_ROWS
    const = lambda bi, i: (0, 0)
    resident = functools.partial(pl.BlockSpec, index_map=const, pipeline_mode=pl.Buffered(1))
    return pl.pallas_call(
        functools.partial(_ffn_out_kernel, final_norm=final_norm),
        grid=(b, s // t),
        in_specs=[
            pl.BlockSpec((1, t, d), lambda bi, i: (bi, i, 0)),
            pl.BlockSpec((1, t, CONV_WIDTH), lambda bi, i: (bi, i, 0)),
            pl.BlockSpec((1, N_HEADS, t, V_DIM), lambda bi, i: (bi, 0, i, 0)),
            resident((d, d)),
            pl.BlockSpec((1, d), const),
            resident((d, 2 * FFN_DIM)),
            pl.BlockSpec((SUBLANES, FFN_DIM), const),
            pl.BlockSpec((1, FFN_DIM), const),
            resident((FFN_DIM, d)),
            pl.BlockSpec((1, d), const),
        ],
        out_specs=pl.BlockSpec((1, t, d), lambda bi, i: (bi, i, 0)),
        out_shape=jax.ShapeDtypeStruct((b, s, d), jnp.float32),
        scratch_shapes=[pltpu.VMEM((FFN_HALO + t, FFN_DIM), jnp.float32)],
        compiler_params=pltpu.CompilerParams(
            dimension_semantics=("arbitrary", "arbitrary"), vmem_limit_bytes=VMEM_LIMIT),
        name="ffn_out",
    )(x, a, o, w_out, g_ffn, w_up, w_fconv, b_fconv, w_down, g_final)


def _rope_tables(seq):
    inv_freq = 1.0 / (ROPE_THETA ** (jnp.arange(0, HEAD_DIM, 2, dtype=jnp.float32) / HEAD_DIM))
    ang = jnp.arange(seq, dtype=jnp.float32)[:, None] * inv_freq[None, :]
    cos, sin = jnp.cos(ang), jnp.sin(ang)
    reps = LANES // HEAD_DIM
    cos_t = jnp.tile(jnp.concatenate([cos, cos], axis=-1), (1, reps))
    sin_t = jnp.tile(jnp.concatenate([-sin, sin], axis=-1), (1, reps))
    return cos_t, sin_t


def _pad_rows(w, rows):
    return jnp.pad(w, ((0, rows - w.shape[0]), (0, 0)))


def kernel(x, g_mix, w_in, w_dw, b_dw, ln_g, ln_b, lambda_q1, lambda_k1, lambda_q2, lambda_k2,
           g_subln, w_out, g_ffn, w_up, w_fconv, b_fconv, w_down, g_final):
    depth = w_in.shape[0]
    seq = x.shape[1]
    cos_t, sin_t = _rope_tables(seq)
    bf = jnp.bfloat16
    row = lambda v: v.reshape(1, -1).astype(jnp.float32)
    for layer in range(depth):
        w_main = w_in[layer][:, :QK_COLS].astype(bf)
        w_vt = w_in[layer][:, QK_COLS:].T.astype(bf)
        a_pre, q, k, vt = _mix_in(x, row(g_mix[layer]), w_main, w_vt, cos_t, sin_t)
        w_taps = jnp.broadcast_to(w_dw[layer].astype(jnp.float32)[:, None, :],
                                  (CONV_KERNEL, SUBLANES, CONV_WIDTH))
        a = _conv_mod(a_pre, w_taps, row(b_dw[layer]), row(ln_g[layer]), row(ln_b[layer]))
        lam_init = 0.8 - 0.6 * math.exp(-0.3 * layer)
        lam_rows = jnp.stack([lambda_q1[layer], lambda_k1[layer], lambda_q2[layer], lambda_k2[layer]])
        lamv = jnp.pad(lam_rows.astype(jnp.float32),
                       ((0, SUBLANES - 4), (0, LANES - lam_rows.shape[1])))
        o = _diff_attn(q, k, vt, lamv, row(g_subln[layer]), lam_init)
        x = _ffn_out(x, a, o, w_out[layer].astype(bf), row(g_ffn[layer]), w_up[layer].astype(bf),
                     _pad_rows(w_fconv[layer], SUBLANES), row(b_fconv[layer]),
                     w_down[layer].astype(bf), row(g_final), final_norm=(layer == depth - 1))
    return x
```

```python
import functools
import math

import jax
import jax.numpy as jnp
from jax import lax
from jax.experimental import pallas as pl
from jax.experimental.pallas import tpu as pltpu

D_MODEL = 1024
CHUNK = 64
CONV_WIDTH = 512
ATTN_WIDTH = 512
N_HEADS = 4
HEAD_DIM = 64
V_DIM = 2 * HEAD_DIM
CONV_KERNEL = 31
FFN_DIM = 2816
FFN_CONV_KERNEL = 3
ROPE_THETA = 10000.0
RMS_EPS = 1e-6
LN_EPS = 1e-5
SUBLN_EPS = 1e-5
QK_COLS = 2 * CONV_WIDTH + 2 * ATTN_WIDTH

LANES = 128
SUBLANES = 8
LOG2E = math.log2(math.e)
NEG_BIG = -1e30

MIX_ROWS = 512
CONV_HALO = 32
CONV_CHUNK = 32
ATT_Q = 512
ATT_K = 512
ATT_UNROLL = 4
ATT_ONES = 16
ATT_SAFE = 2.0 ** 60
FFN_ROWS = 512
FFN_SPLIT = 2
FFN_HALO = SUBLANES
FFN_BLOCK = 256

VMEM_LIMIT = 56 * 1024 * 1024


def _nt_dot(a, b):
    return lax.dot_general(a, b, (((1,), (1,)), ((), ())), preferred_element_type=jnp.float32)


def _nn_dot(a, b):
    return jnp.dot(a, b, preferred_element_type=jnp.float32)


def _sigmoid(x):
    return 1.0 / (1.0 + jnp.exp(-x))


def _mix_in_kernel(x_ref, g_ref, w_ref, wvt_ref, cos_ref, sin_ref, wdw_ref, bdw_ref, lng_ref,
                   lnb_ref, q_ref, k_ref, vt_ref, a_ref, hist_ref, cur_ref, u_ref, *, q_scale):
    i = pl.program_id(1)
    rows = x_ref.shape[1]

    @pl.when(i == 0)
    def _():
        hist_ref[...] = jnp.zeros(hist_ref.shape, jnp.float32)

    bias = bdw_ref[...]
    lng = lng_ref[...]
    lnb = lnb_ref[...]
    first_tap = CONV_HALO - (CONV_KERNEL - 1)
    groups = CONV_CHUNK // SUBLANES
    n_chunks = rows // CONV_CHUNK

    def shift_slab(sl):
        r0 = sl * CONV_CHUNK
        n = min(CONV_CHUNK, CONV_HALO + rows - SUBLANES - r0)
        for r in range(1, SUBLANES):
            u_ref[r - 1, r0:r0 + n, :] = hist_ref[r0 + r:r0 + r + n, :]

    def conv_chunk(c):
        shift_slab(c + 1)
        base = c * CONV_CHUNK
        accs = [jnp.broadcast_to(bias, (CONV_CHUNK, CONV_WIDTH)), None]
        for tap in range(CONV_KERNEL):
            off = first_tap + tap
            aligned = base + (off // SUBLANES) * SUBLANES
            w_tap = jnp.concatenate([wdw_ref[tap]] * groups, axis=0)
            if off % SUBLANES == 0:
                u = hist_ref[aligned:aligned + CONV_CHUNK, :]
            else:
                u = u_ref[off % SUBLANES - 1, aligned:aligned + CONV_CHUNK, :]
            term = w_tap * u
            accs[tap % 2] = term if accs[tap % 2] is None else accs[tap % 2] + term
        acc = accs[0] + accs[1]
        mu = jnp.mean(acc, axis=-1, keepdims=True)
        cen = acc - mu
        var = jnp.mean(cen * cen, axis=-1, keepdims=True)
        y = cen * lax.rsqrt(var + LN_EPS) * lng + lnb
        a_ref[0, base:base + CONV_CHUNK, :] = (y * _sigmoid(y)).astype(a_ref.dtype)

    chunk_iter = iter(range(n_chunks))

    def conv_chunks(count):
        for _ in range(count):
            c = next(chunk_iter, None)
            if c is not None:
                conv_chunk(c)

    per_piece = -(-n_chunks // 5)

    x = x_ref[0]
    ms = jnp.mean(x * x, axis=-1, keepdims=True)
    h = (x * lax.rsqrt(ms + RMS_EPS) * g_ref[...]).astype(jnp.bfloat16)
    shift_slab(0)

    c_val = _nn_dot(h, w_ref[:, :CONV_WIDTH])
    conv_chunks(per_piece)
    c_gate = _nn_dot(h, w_ref[:, CONV_WIDTH:2 * CONV_WIDTH])
    cur_ref[...] = c_val * _sigmoid(c_gate)
    conv_chunks(per_piece)

    cos = cos_ref[...]
    sin = sin_ref[...]
    lane = lax.broadcasted_iota(jnp.int32, cos.shape, 1)
    first_half = (lane % HEAD_DIM) < (HEAD_DIM // 2)

    def rope(t):
        up = pltpu.roll(t, LANES - HEAD_DIM // 2, axis=1)
        down = pltpu.roll(t, HEAD_DIM // 2, axis=1)
        return t * cos + jnp.where(first_half, up, down) * sin

    q0 = 2 * CONV_WIDTH
    k0 = q0 + ATTN_WIDTH
    qm = _nn_dot(h, w_ref[:, q0:q0 + ATTN_WIDTH])
    for hd in range(N_HEADS):
        q_ref[0, hd] = (rope(qm[:, hd * V_DIM:(hd + 1) * V_DIM]) * q_scale).astype(jnp.bfloat16)
    conv_chunks(per_piece)
    km = _nn_dot(h, w_ref[:, k0:k0 + ATTN_WIDTH])
    for hd in range(N_HEADS):
        k_ref[0, hd] = rope(km[:, hd * V_DIM:(hd + 1) * V_DIM]).astype(jnp.bfloat16)
    conv_chunks(per_piece)

    vt = _nt_dot(wvt_ref[...], h).astype(jnp.bfloat16)
    ones = jnp.ones((ATT_ONES, ATT_K), jnp.bfloat16)
    for j in range(rows // ATT_K):
        for hd in range(N_HEADS):
            vt_ref[0, j, hd, 0:V_DIM, :] = vt[hd * V_DIM:(hd + 1) * V_DIM, j * ATT_K:(j + 1) * ATT_K]
            vt_ref[0, j, hd, V_DIM:V_DIM + ATT_ONES, :] = ones
    conv_chunks(n_chunks)

    hist_ref[0:CONV_HALO, :] = hist_ref[rows:rows + CONV_HALO, :]
    hist_ref[CONV_HALO:CONV_HALO + rows, :] = cur_ref[...]


def _mix_in(x, g_mix, w_main, w_vt, cos_t, sin_t, w_taps, b_dw, ln_g, ln_b):
    b, s, d = x.shape
    t = MIX_ROWS
    n = s // t
    q_scale = HEAD_DIM ** -0.5 * LOG2E
    const = lambda bi, i: (0, 0)
    proj = lambda i: jnp.minimum(i, n - 1)
    conv = lambda i: jnp.maximum(i - 1, 0)
    return pl.pallas_call(
        functools.partial(_mix_in_kernel, q_scale=q_scale),
        grid=(b, n + 1),
        in_specs=[
            pl.BlockSpec((1, t, d), lambda bi, i: (bi, proj(i), 0)),
            pl.BlockSpec((1, d), const),
            pl.BlockSpec((d, QK_COLS), const),
            pl.BlockSpec((ATTN_WIDTH, d), const),
            pl.BlockSpec((t, LANES), lambda bi, i: (proj(i), 0)),
            pl.BlockSpec((t, LANES), lambda bi, i: (proj(i), 0)),
            pl.BlockSpec((CONV_KERNEL, SUBLANES, CONV_WIDTH), lambda bi, i: (0, 0, 0)),
            pl.BlockSpec((1, CONV_WIDTH), const),
            pl.BlockSpec((1, CONV_WIDTH), const),
            pl.BlockSpec((1, CONV_WIDTH), const),
        ],
        out_specs=[
            pl.BlockSpec((1, N_HEADS, t, V_DIM), lambda bi, i: (bi, 0, proj(i), 0)),
            pl.BlockSpec((1, N_HEADS, t, V_DIM), lambda bi, i: (bi, 0, proj(i), 0)),
            pl.BlockSpec((1, t // ATT_K, N_HEADS, V_DIM + ATT_ONES, ATT_K),
                         lambda bi, i: (bi, proj(i), 0, 0, 0)),
            pl.BlockSpec((1, t, CONV_WIDTH), lambda bi, i: (bi, conv(i), 0)),
        ],
        out_shape=[
            jax.ShapeDtypeStruct((b, N_HEADS, s, V_DIM), jnp.bfloat16),
            jax.ShapeDtypeStruct((b, N_HEADS, s, V_DIM), jnp.bfloat16),
            jax.ShapeDtypeStruct((b, s // ATT_K, N_HEADS, V_DIM + ATT_ONES, ATT_K), jnp.bfloat16),
            jax.ShapeDtypeStruct((b, s, CONV_WIDTH), jnp.bfloat16),
        ],
        scratch_shapes=[
            pltpu.VMEM((CONV_HALO + t, CONV_WIDTH), jnp.float32),
            pltpu.VMEM((t, CONV_WIDTH), jnp.float32),
            pltpu.VMEM((SUBLANES - 1, CONV_HALO + t, CONV_WIDTH), jnp.float32),
        ],
        compiler_params=pltpu.CompilerParams(
            dimension_semantics=("arbitrary", "arbitrary"), vmem_limit_bytes=VMEM_LIMIT),
        name="mix_in",
    )(x, g_mix, w_main, w_vt, cos_t, sin_t, w_taps, b_dw, ln_g, ln_b)


def _diff_attn_kernel(q_ref, k_ref, vt_ref, lamv_ref, g_ref, o_ref,
                      m_ref, acc_ref, qq_ref, sa_ref, sb_ref, mask_ref, *, lam_init):
    seq = q_ref.shape[2]
    tq, tk = ATT_Q, ATT_K
    lamv = lamv_ref[...]
    lam = (jnp.exp(jnp.sum(lamv[0:1] * lamv[1:2], axis=-1, keepdims=True))
           - jnp.exp(jnp.sum(lamv[2:3] * lamv[3:4], axis=-1, keepdims=True)) + lam_init)
    gain = g_ref[...] * (1.0 - lam_init)

    lane = lax.broadcasted_iota(jnp.int32, (tq, V_DIM), 1)
    key_chunk = lax.broadcasted_iota(jnp.int32, (tk, 2 * tq), 0) // CHUNK
    qry_chunk = (lax.broadcasted_iota(jnp.int32, (tk, 2 * tq), 1) % tq) // CHUNK
    @pl.when(jnp.logical_and(pl.program_id(0) == 0, pl.program_id(1) == 0))
    def _():
        mask_ref[...] = jnp.where(key_chunk <= qry_chunk, 0.0, NEG_BIG)

    def scores(ki, s_ref):
        k0 = pl.multiple_of(ki * tk, tk)
        s_ref[...] = _nt_dot(k_ref[0, 0, pl.ds(k0, tk), :], qq_ref[...])

    def values(ki, p):
        return _nn_dot(vt_ref[0, ki, 0], p.astype(jnp.bfloat16))

    def anchor(qi, s_ref):
        m_ref[...] = jnp.max(s_ref[0:CHUNK, :], axis=0, keepdims=True)
        acc_ref[...] = values(qi, jnp.exp2(s_ref[...] + mask_ref[...] - m_ref[...]))

    def add_tile(ki, s_ref):
        acc_ref[...] += values(ki, jnp.exp2(s_ref[...] - m_ref[...]))

    def online_tile(ki, s_ref, masked):
        s = s_ref[...]
        if masked:
            s = s + mask_ref[...]
        m_old = m_ref[...]
        m_new = jnp.maximum(m_old, jnp.max(s, axis=0, keepdims=True))
        acc_ref[...] = jnp.exp2(m_old - m_new) * acc_ref[...] + values(ki, jnp.exp2(s - m_new))
        m_ref[...] = m_new

    def settle(tile):
        denom = acc_ref[V_DIM:V_DIM + 1, :]
        in_range = jnp.logical_and(denom < ATT_SAFE, denom > 1.0 / ATT_SAFE)
        overflowed = jnp.max(jnp.where(in_range, 0.0, 1.0)) > 0.5

        @pl.when(overflowed)
        def _():
            m_ref[...] = jnp.full(m_ref.shape, NEG_BIG, jnp.float32)
            acc_ref[...] = jnp.zeros(acc_ref.shape, jnp.float32)

            def step(ki, c):
                scores(ki, sa_ref)
                online_tile(ki, sa_ref, False)
                return c

            lax.fori_loop(0, tile, step, 0)
            scores(tile, sa_ref)
            online_tile(tile, sa_ref, True)

    def emit(tile):
        acc = acc_ref[...]
        inv_l = 1.0 / acc[V_DIM:V_DIM + 1, :]
        o_t = (acc[:V_DIM, :tq] * inv_l[:, :tq]
               - lam * (acc[:V_DIM, tq:] * inv_l[:, tq:]))
        o = o_t.T
        ms = jnp.mean(o * o, axis=-1, keepdims=True)
        r0 = pl.multiple_of(tile * tq, tq)
        o_ref[0, 0, pl.ds(r0, tq), :] = (o * lax.rsqrt(ms + SUBLN_EPS) * gain).astype(o_ref.dtype)

    def q_tile(qi, carry):
        prev = jnp.maximum(qi - 1, 0)
        settle(prev)
        emit(prev)

        q0 = pl.multiple_of(qi * tq, tq)
        qt = q_ref[0, 0, pl.ds(q0, tq), :]
        zero = jnp.zeros_like(qt)
        qq_ref[0:tq, :] = jnp.where(lane < HEAD_DIM, qt, zero)
        qq_ref[tq:2 * tq, :] = jnp.where(lane >= HEAD_DIM, qt, zero)

        scores(qi, sb_ref)
        scores(0, sa_ref)
        anchor(qi, sb_ref)

        bufs = (sa_ref, sb_ref)

        def run(first, count, lookahead):
            for j in range(count):
                if j + 1 < count or lookahead:
                    scores(first + j + 1, bufs[(j + 1) % 2])
                add_tile(first + j, bufs[j % 2])

        n_groups = (qi - 1) // ATT_UNROLL

        def group(i, c):
            run(ATT_UNROLL * i, ATT_UNROLL, True)
            return c

        lax.fori_loop(0, n_groups, group, 0)

        left = qi - ATT_UNROLL * jnp.maximum(n_groups, 0)
        for r in range(1, ATT_UNROLL + 1):
            @pl.when(left == r)
            def _(r=r):
                run(qi - r, r, False)

        return carry

    n_q = seq // tq
    acc_ref[...] = jnp.ones(acc_ref.shape, jnp.float32)
    lax.fori_loop(0, n_q, q_tile, 0)
    settle(n_q - 1)
    emit(n_q - 1)


def _diff_attn(q, k, vt, lamv, g_subln, lam_init):
    b, nh, s, dv = q.shape
    return pl.pallas_call(
        functools.partial(_diff_attn_kernel, lam_init=lam_init),
        grid=(b, nh),
        in_specs=[
            pl.BlockSpec((1, 1, s, dv), lambda bi, hi: (bi, hi, 0, 0)),
            pl.BlockSpec((1, 1, s, dv), lambda bi, hi: (bi, hi, 0, 0)),
            pl.BlockSpec((1, s // ATT_K, 1, dv + ATT_ONES, ATT_K), lambda bi, hi: (bi, 0, hi, 0, 0)),
            pl.BlockSpec((SUBLANES, LANES), lambda bi, hi: (0, 0)),
            pl.BlockSpec((1, dv), lambda bi, hi: (0, 0)),
        ],
        out_specs=pl.BlockSpec((1, 1, s, dv), lambda bi, hi: (bi, hi, 0, 0)),
        out_shape=jax.ShapeDtypeStruct((b, nh, s, dv), jnp.bfloat16),
        scratch_shapes=[
            pltpu.VMEM((1, 2 * ATT_Q), jnp.float32),
            pltpu.VMEM((dv + ATT_ONES, 2 * ATT_Q), jnp.float32),
            pltpu.VMEM((2 * ATT_Q, dv), jnp.bfloat16),
            pltpu.VMEM((ATT_K, 2 * ATT_Q), jnp.float32),
            pltpu.VMEM((ATT_K, 2 * ATT_Q), jnp.float32),
            pltpu.VMEM((ATT_K, 2 * ATT_Q), jnp.float32),
        ],
        compiler_params=pltpu.CompilerParams(
            dimension_semantics=("arbitrary", "arbitrary"), vmem_limit_bytes=VMEM_LIMIT),
        name="diff_attn",
    )(q, k, vt, lamv, g_subln)


def _rms(x, g):
    ms = jnp.mean(x * x, axis=-1, keepdims=True)
    return x * lax.rsqrt(ms + RMS_EPS) * g


def _ffn_out_kernel(x_ref, a_ref, o_ref, wout_ref, gffn_ref, wup_ref, wf_ref, bf_ref, wdn_ref,
                    gfin_ref, y_ref, gate_ref, *, final_norm):
    i = pl.program_id(1)
    rows = x_ref.shape[1]
    sub = rows // FFN_SPLIT
    parts = range(FFN_SPLIT)
    n_blocks = FFN_DIM // FFN_BLOCK

    x1 = []
    for s in parts:
        r = slice(s * sub, (s + 1) * sub)
        o_cat = jnp.concatenate([o_ref[0, hd, r, :] for hd in range(N_HEADS)], axis=-1)
        ao = jnp.concatenate([a_ref[0, r, :], o_cat], axis=-1)
        x1.append(x_ref[0, r, :] + _nn_dot(ao, wout_ref[...]))
    h = [_rms(x1[s], gffn_ref[...]).astype(jnp.bfloat16) for s in parts]

    @pl.when(i == 0)
    def _():
        gate_ref[0:FFN_HALO, :] = jnp.zeros((FFN_HALO, FFN_DIM), jnp.float32)

    def up_proj(s, j):
        gate_j = _nn_dot(h[s], wup_ref[:, j * FFN_BLOCK:(j + 1) * FFN_BLOCK])
        val_j = _nn_dot(h[s], wup_ref[:, FFN_DIM + j * FFN_BLOCK:FFN_DIM + (j + 1) * FFN_BLOCK])
        return gate_j, val_j

    x2 = list(x1)
    up_next = [up_proj(s, 0) for s in parts]
    for j in range(n_blocks):
        c0 = j * FFN_BLOCK
        cols = slice(c0, c0 + FFN_BLOCK)
        for s in parts:
            gate_j, val_j = up_next[s]
            if j + 1 < n_blocks:
                up_next[s] = up_proj(s, j + 1)
            base = FFN_HALO + s * sub
            gate_ref[base:base + sub, cols] = gate_j
            conv = bf_ref[:, cols]
            for tap in range(FFN_CONV_KERNEL):
                off = base - (FFN_CONV_KERNEL - 1) + tap
                conv = conv + wf_ref[tap:tap + 1, cols] * gate_ref[off:off + sub, cols]
            hidden = (conv * _sigmoid(conv) * val_j).astype(jnp.bfloat16)
            x2[s] = x2[s] + _nn_dot(hidden, wdn_ref[cols, :])
        carry_rows = gate_ref[rows:rows + FFN_HALO, cols]
        gate_ref[0:FFN_HALO, cols] = carry_rows
    for s in parts:
        out = _rms(x2[s], gfin_ref[...]) if final_norm else x2[s]
        y_ref[0, s * sub:(s + 1) * sub, :] = out


def _ffn_out(x, a, o, w_out, g_ffn, w_up, w_fconv, b_fconv, w_down, g_final, final_norm):
    b, s, d = x.shape
    t = FFN_ROWS
    const = lambda bi, i: (0, 0)
    resident = functools.partial(pl.BlockSpec, index_map=const, pipeline_mode=pl.Buffered(1))
    return pl.pallas_call(
        functools.partial(_ffn_out_kernel, final_norm=final_norm),
        grid=(b, s // t),
        in_specs=[
            pl.BlockSpec((1, t, d), lambda bi, i: (bi, i, 0)),
            pl.BlockSpec((1, t, CONV_WIDTH), lambda bi, i: (bi, i, 0)),
            pl.BlockSpec((1, N_HEADS, t, V_DIM), lambda bi, i: (bi, 0, i, 0)),
            resident((d, d)),
            pl.BlockSpec((1, d), const),
            resident((d, 2 * FFN_DIM)),
            pl.BlockSpec((SUBLANES, FFN_DIM), const),
            pl.BlockSpec((1, FFN_DIM), const),
            resident((FFN_DIM, d)),
            pl.BlockSpec((1, d), const),
        ],
        out_specs=pl.BlockSpec((1, t, d), lambda bi, i: (bi, i, 0)),
        out_shape=jax.ShapeDtypeStruct((b, s, d), jnp.float32),
        scratch_shapes=[pltpu.VMEM((FFN_HALO + t, FFN_DIM), jnp.float32)],
        compiler_params=pltpu.CompilerParams(
            dimension_semantics=("arbitrary", "arbitrary"), vmem_limit_bytes=VMEM_LIMIT),
        name="ffn_out",
    )(x, a, o, w_out, g_ffn, w_up, w_fconv, b_fconv, w_down, g_final)


def _rope_tables(seq):
    inv_freq = 1.0 / (ROPE_THETA ** (jnp.arange(0, HEAD_DIM, 2, dtype=jnp.float32) / HEAD_DIM))
    ang = jnp.arange(seq, dtype=jnp.float32)[:, None] * inv_freq[None, :]
    cos, sin = jnp.cos(ang), jnp.sin(ang)
    reps = LANES // HEAD_DIM
    cos_t = jnp.tile(jnp.concatenate([cos, cos], axis=-1), (1, reps))
    sin_t = jnp.tile(jnp.concatenate([-sin, sin], axis=-1), (1, reps))
    return cos_t, sin_t


def _pad_rows(w, rows):
    return jnp.pad(w, ((0, rows - w.shape[0]), (0, 0)))


def kernel(x, g_mix, w_in, w_dw, b_dw, ln_g, ln_b, lambda_q1, lambda_k1, lambda_q2, lambda_k2,
           g_subln, w_out, g_ffn, w_up, w_fconv, b_fconv, w_down, g_final):
    depth = w_in.shape[0]
    seq = x.shape[1]
    cos_t, sin_t = _rope_tables(seq)
    bf = jnp.bfloat16
    row = lambda v: v.reshape(1, -1).astype(jnp.float32)
    for layer in range(depth):
        w_main = w_in[layer][:, :QK_COLS].astype(bf)
        w_vt = w_in[layer][:, QK_COLS:].T.astype(bf)
        w_taps = jnp.broadcast_to(w_dw[layer].astype(jnp.float32)[:, None, :],
                                  (CONV_KERNEL, SUBLANES, CONV_WIDTH))
        q, k, vt, a = _mix_in(x, row(g_mix[layer]), w_main, w_vt, cos_t, sin_t, w_taps,
                              row(b_dw[layer]), row(ln_g[layer]), row(ln_b[layer]))
        lam_init = 0.8 - 0.6 * math.exp(-0.3 * layer)
        lam_rows = jnp.stack([lambda_q1[layer], lambda_k1[layer], lambda_q2[layer], lambda_k2[layer]])
        lamv = jnp.pad(lam_rows.astype(jnp.float32),
                       ((0, SUBLANES - 4), (0, LANES - lam_rows.shape[1])))
        o = _diff_attn(q, k, vt, lamv, row(g_subln[layer]), lam_init)
        x = _ffn_out(x, a, o, w_out[layer].astype(bf), row(g_ffn[layer]), w_up[layer].astype(bf),
                     _pad_rows(w_fconv[layer], SUBLANES), row(b_fconv[layer]),
                     w_down[layer].astype(bf), row(g_final), final_norm=(layer == depth - 1))
    return x
```

```python
import functools
import math

import jax
import jax.numpy as jnp
from jax import lax
from jax.experimental import pallas as pl
from jax.experimental.pallas import tpu as pltpu

D_MODEL = 1024
CHUNK = 64
CONV_WIDTH = 512
ATTN_WIDTH = 512
N_HEADS = 4
HEAD_DIM = 64
V_DIM = 2 * HEAD_DIM
CONV_KERNEL = 31
FFN_DIM = 2816
FFN_CONV_KERNEL = 3
ROPE_THETA = 10000.0
RMS_EPS = 1e-6
LN_EPS = 1e-5
SUBLN_EPS = 1e-5
QK_COLS = 2 * CONV_WIDTH + 2 * ATTN_WIDTH

LANES = 128
SUBLANES = 8
LOG2E = math.log2(math.e)
NEG_BIG = -1e30

MIX_ROWS = 512
MIX_SPLIT = 2
CONV_ROWS = 512
CONV_HALO = 32
CONV_CHUNK = 32
ATT_Q = 512
ATT_K = 512
ATT_UNROLL = 6
ATT_ONES = 16
ATT_SAFE = 2.0 ** 60
FFN_ROWS = 512
FFN_SPLIT = 2
FFN_HALO = SUBLANES
FFN_BLOCK = 256

VMEM_LIMIT = 56 * 1024 * 1024


def _nt_dot(a, b):
    return lax.dot_general(a, b, (((1,), (1,)), ((), ())), preferred_element_type=jnp.float32)


def _nn_dot(a, b):
    return jnp.dot(a, b, preferred_element_type=jnp.float32)


def _sigmoid(x):
    return 1.0 / (1.0 + jnp.exp(-x))


def _mix_in_kernel(x_ref, g_ref, w_ref, wvt_ref, cos_ref, sin_ref,
                   apre_ref, q_ref, k_ref, vt_ref, *, q_scale):
    rows = x_ref.shape[1]
    sub = rows // MIX_SPLIT
    lane = lax.broadcasted_iota(jnp.int32, (sub, LANES), 1)
    first_half = (lane % HEAD_DIM) < (HEAD_DIM // 2)
    q0 = 2 * CONV_WIDTH
    k0 = q0 + ATTN_WIDTH
    ones = jnp.ones((ATT_ONES, sub), jnp.bfloat16)

    h = []
    for s in range(MIX_SPLIT):
        x = x_ref[0, s * sub:(s + 1) * sub, :]
        ms = jnp.mean(x * x, axis=-1, keepdims=True)
        h.append((x * lax.rsqrt(ms + RMS_EPS) * g_ref[...]).astype(jnp.bfloat16))
    main = [_nn_dot(h[s], w_ref[...]) for s in range(MIX_SPLIT)]

    for s in range(MIX_SPLIT):
        r = slice(s * sub, (s + 1) * sub)
        c_val = main[s][:, :CONV_WIDTH]
        c_gate = main[s][:, CONV_WIDTH:2 * CONV_WIDTH]
        apre_ref[0, r, :] = c_val * _sigmoid(c_gate)

        cos = cos_ref[r, :]
        sin = sin_ref[r, :]

        def rope(t):
            up = pltpu.roll(t, LANES - HEAD_DIM // 2, axis=1)
            down = pltpu.roll(t, HEAD_DIM // 2, axis=1)
            return t * cos + jnp.where(first_half, up, down) * sin

        for hd in range(N_HEADS):
            qh = main[s][:, q0 + hd * V_DIM:q0 + (hd + 1) * V_DIM]
            kh = main[s][:, k0 + hd * V_DIM:k0 + (hd + 1) * V_DIM]
            q_ref[0, hd, r, :] = (rope(qh) * q_scale).astype(jnp.bfloat16)
            k_ref[0, hd, r, :] = rope(kh).astype(jnp.bfloat16)

    for s in range(MIX_SPLIT):
        vt = _nt_dot(wvt_ref[...], h[s]).astype(jnp.bfloat16)
        j, c0 = divmod(s * sub, ATT_K)
        for hd in range(N_HEADS):
            vt_ref[0, j, hd, 0:V_DIM, c0:c0 + sub] = vt[hd * V_DIM:(hd + 1) * V_DIM, :]
            vt_ref[0, j, hd, V_DIM:V_DIM + ATT_ONES, c0:c0 + sub] = ones


def _mix_in(x, g_mix, w_main, w_vt, cos_t, sin_t):
    b, s, d = x.shape
    t = MIX_ROWS
    q_scale = HEAD_DIM ** -0.5 * LOG2E
    const = lambda bi, i: (0, 0)
    return pl.pallas_call(
        functools.partial(_mix_in_kernel, q_scale=q_scale),
        grid=(b, s // t),
        in_specs=[
            pl.BlockSpec((1, t, d), lambda bi, i: (bi, i, 0)),
            pl.BlockSpec((1, d), const),
            pl.BlockSpec((d, QK_COLS), const),
            pl.BlockSpec((ATTN_WIDTH, d), const),
            pl.BlockSpec((t, LANES), lambda bi, i: (i, 0)),
            pl.BlockSpec((t, LANES), lambda bi, i: (i, 0)),
        ],
        out_specs=[
            pl.BlockSpec((1, t, CONV_WIDTH), lambda bi, i: (bi, i, 0)),
            pl.BlockSpec((1, N_HEADS, t, V_DIM), lambda bi, i: (bi, 0, i, 0)),
            pl.BlockSpec((1, N_HEADS, t, V_DIM), lambda bi, i: (bi, 0, i, 0)),
            pl.BlockSpec((1, t // ATT_K, N_HEADS, V_DIM + ATT_ONES, ATT_K),
                         lambda bi, i: (bi, i, 0, 0, 0)),
        ],
        out_shape=[
            jax.ShapeDtypeStruct((b, s, CONV_WIDTH), jnp.float32),
            jax.ShapeDtypeStruct((b, N_HEADS, s, V_DIM), jnp.bfloat16),
            jax.ShapeDtypeStruct((b, N_HEADS, s, V_DIM), jnp.bfloat16),
            jax.ShapeDtypeStruct((b, s // ATT_K, N_HEADS, V_DIM + ATT_ONES, ATT_K), jnp.bfloat16),
        ],
        compiler_params=pltpu.CompilerParams(
            dimension_semantics=("arbitrary", "arbitrary"), vmem_limit_bytes=VMEM_LIMIT),
        name="mix_in",
    )(x, g_mix, w_main, w_vt, cos_t, sin_t)


def _conv_mod_kernel(cur_ref, halo_ref, w_ref, b_ref, lng_ref, lnb_ref, a_ref, u_ref):
    i = pl.program_id(1)
    rows = cur_ref.shape[1]
    halo = halo_ref[0]
    u_ref[0, 0:CONV_HALO, :] = jnp.where(i == 0, jnp.zeros_like(halo), halo)
    u_ref[0, CONV_HALO:CONV_HALO + rows, :] = cur_ref[0]
    span = CONV_HALO + rows - SUBLANES
    for r in range(1, SUBLANES):
        u_ref[r, 0:span, :] = u_ref[0, r:r + span, :]

    bias = b_ref[...]
    lng = lng_ref[...]
    lnb = lnb_ref[...]
    first_tap = CONV_HALO - (CONV_KERNEL - 1)

    groups = CONV_CHUNK // SUBLANES

    def chunk(c, carry):
        base = c * CONV_CHUNK
        accs = [jnp.broadcast_to(bias, (CONV_CHUNK, CONV_WIDTH)), None]
        for tap in range(CONV_KERNEL):
            off = first_tap + tap
            aligned = base + (off // SUBLANES) * SUBLANES
            w_tap = jnp.concatenate([w_ref[tap]] * groups, axis=0)
            term = w_tap * u_ref[off % SUBLANES, pl.ds(aligned, CONV_CHUNK), :]
            accs[tap % 2] = term if accs[tap % 2] is None else accs[tap % 2] + term
        acc = accs[0] + accs[1]
        mu = jnp.mean(acc, axis=-1, keepdims=True)
        cen = acc - mu
        var = jnp.mean(cen * cen, axis=-1, keepdims=True)
        y = cen * lax.rsqrt(var + LN_EPS) * lng + lnb
        a_ref[0, pl.ds(base, CONV_CHUNK), :] = (y * _sigmoid(y)).astype(a_ref.dtype)
        return carry

    for c in range(rows // CONV_CHUNK):
        chunk(c, 0)


def _conv_mod(a_pre, w_dw, b_dw, ln_g, ln_b):
    b, s, c = a_pre.shape
    t = CONV_ROWS
    per = t // CONV_HALO
    const = lambda bi, i: (0, 0)
    return pl.pallas_call(
        _conv_mod_kernel,
        grid=(b, s // t),
        in_specs=[
            pl.BlockSpec((1, t, c), lambda bi, i: (bi, i, 0)),
            pl.BlockSpec((1, CONV_HALO, c), lambda bi, i: (bi, jnp.maximum(i * per - 1, 0), 0)),
            pl.BlockSpec((CONV_KERNEL, SUBLANES, c), lambda bi, i: (0, 0, 0)),
            pl.BlockSpec((1, c), const),
            pl.BlockSpec((1, c), const),
            pl.BlockSpec((1, c), const),
        ],
        out_specs=pl.BlockSpec((1, t, c), lambda bi, i: (bi, i, 0)),
        out_shape=jax.ShapeDtypeStruct((b, s, c), jnp.bfloat16),
        scratch_shapes=[pltpu.VMEM((SUBLANES, CONV_HALO + t, c), jnp.float32)],
        compiler_params=pltpu.CompilerParams(
            dimension_semantics=("arbitrary", "arbitrary"), vmem_limit_bytes=VMEM_LIMIT),
        name="conv_mod",
    )(a_pre, a_pre, w_dw, b_dw, ln_g, ln_b)


def _diff_attn_kernel(q_ref, k_ref, vt_ref, lamv_ref, g_ref, o_ref,
                      m_ref, acc_ref, qq_ref, sa_ref, sb_ref, mask_ref, *, lam_init):
    seq = q_ref.shape[2]
    tq, tk = ATT_Q, ATT_K
    lamv = lamv_ref[...]
    lam = (jnp.exp(jnp.sum(lamv[0:1] * lamv[1:2], axis=-1, keepdims=True))
           - jnp.exp(jnp.sum(lamv[2:3] * lamv[3:4], axis=-1, keepdims=True)) + lam_init)
    gain = g_ref[...] * (1.0 - lam_init)

    lane = lax.broadcasted_iota(jnp.int32, (tq, V_DIM), 1)
    key_chunk = lax.broadcasted_iota(jnp.int32, (tk, 2 * tq), 0) // CHUNK
    qry_chunk = (lax.broadcasted_iota(jnp.int32, (tk, 2 * tq), 1) % tq) // CHUNK
    @pl.when(jnp.logical_and(pl.program_id(0) == 0, pl.program_id(1) == 0))
    def _():
        mask_ref[...] = jnp.where(key_chunk <= qry_chunk, 0.0, NEG_BIG)

    def scores(ki, s_ref):
        k0 = pl.multiple_of(ki * tk, tk)
        s_ref[...] = _nt_dot(k_ref[0, 0, pl.ds(k0, tk), :], qq_ref[...])

    def values(ki, p):
        return _nn_dot(vt_ref[0, ki, 0], p.astype(jnp.bfloat16))

    def anchor(qi, s_ref):
        m_ref[...] = jnp.max(s_ref[0:CHUNK, :], axis=0, keepdims=True)
        acc_ref[...] = values(qi, jnp.exp2(s_ref[...] + mask_ref[...] - m_ref[...]))

    def add_tile(ki, s_ref):
        acc_ref[...] += values(ki, jnp.exp2(s_ref[...] - m_ref[...]))

    def online_tile(ki, s_ref, masked):
        s = s_ref[...]
        if masked:
            s = s + mask_ref[...]
        m_old = m_ref[...]
        m_new = jnp.maximum(m_old, jnp.max(s, axis=0, keepdims=True))
        acc_ref[...] = jnp.exp2(m_old - m_new) * acc_ref[...] + values(ki, jnp.exp2(s - m_new))
        m_ref[...] = m_new

    def settle(tile):
        denom = acc_ref[V_DIM:V_DIM + 1, :]
        in_range = jnp.logical_and(denom < ATT_SAFE, denom > 1.0 / ATT_SAFE)
        overflowed = jnp.max(jnp.where(in_range, 0.0, 1.0)) > 0.5

        @pl.when(overflowed)
        def _():
            m_ref[...] = jnp.full(m_ref.shape, NEG_BIG, jnp.float32)
            acc_ref[...] = jnp.zeros(acc_ref.shape, jnp.float32)

            def step(ki, c):
                scores(ki, sa_ref)
                online_tile(ki, sa_ref, False)
                return c

            lax.fori_loop(0, tile, step, 0)
            scores(tile, sa_ref)
            online_tile(tile, sa_ref, True)

    def emit(tile):
        acc = acc_ref[...]
        inv_l = 1.0 / acc[V_DIM:V_DIM + 1, :]
        o_t = (acc[:V_DIM, :tq] * inv_l[:, :tq]
               - lam * (acc[:V_DIM, tq:] * inv_l[:, tq:]))
        o = o_t.T
        ms = jnp.mean(o * o, axis=-1, keepdims=True)
        r0 = pl.multiple_of(tile * tq, tq)
        o_ref[0, 0, pl.ds(r0, tq), :] = (o * lax.rsqrt(ms + SUBLN_EPS) * gain).astype(o_ref.dtype)

    def q_tile(qi, carry):
        prev = jnp.maximum(qi - 1, 0)
        settle(prev)
        emit(prev)

        q0 = pl.multiple_of(qi * tq, tq)
        qt = q_ref[0, 0, pl.ds(q0, tq), :]
        zero = jnp.zeros_like(qt)
        qq_ref[0:tq, :] = jnp.where(lane < HEAD_DIM, qt, zero)
        qq_ref[tq:2 * tq, :] = jnp.where(lane >= HEAD_DIM, qt, zero)

        scores(qi, sb_ref)
        scores(0, sa_ref)
        anchor(qi, sb_ref)

        bufs = (sa_ref, sb_ref)

        def run(first, count, lookahead):
            for j in range(count):
                if j + 1 < count or lookahead:
                    scores(first + j + 1, bufs[(j + 1) % 2])
                add_tile(first + j, bufs[j % 2])

        n_groups = (qi - 1) // ATT_UNROLL

        def group(i, c):
            run(ATT_UNROLL * i, ATT_UNROLL, True)
            return c

        lax.fori_loop(0, n_groups, group, 0)

        left = qi - ATT_UNROLL * jnp.maximum(n_groups, 0)
        for r in range(1, ATT_UNROLL + 1):
            @pl.when(left == r)
            def _(r=r):
                run(qi - r, r, False)

        return carry

    n_q = seq // tq
    acc_ref[...] = jnp.ones(acc_ref.shape, jnp.float32)
    lax.fori_loop(0, n_q, q_tile, 0)
    settle(n_q - 1)
    emit(n_q - 1)


def _diff_attn(q, k, vt, lamv, g_subln, lam_init):
    b, nh, s, dv = q.shape
    return pl.pallas_call(
        functools.partial(_diff_attn_kernel, lam_init=lam_init),
        grid=(b, nh),
        in_specs=[
            pl.BlockSpec((1, 1, s, dv), lambda bi, hi: (bi, hi, 0, 0)),
            pl.BlockSpec((1, 1, s, dv), lambda bi, hi: (bi, hi, 0, 0)),
            pl.BlockSpec((1, s // ATT_K, 1, dv + ATT_ONES, ATT_K), lambda bi, hi: (bi, 0, hi, 0, 0)),
            pl.BlockSpec((SUBLANES, LANES), lambda bi, hi: (0, 0)),
            pl.BlockSpec((1, dv), lambda bi, hi: (0, 0)),
        ],
        out_specs=pl.BlockSpec((1, 1, s, dv), lambda bi, hi: (bi, hi, 0, 0)),
        out_shape=jax.ShapeDtypeStruct((b, nh, s, dv), jnp.bfloat16),
        scratch_shapes=[
            pltpu.VMEM((1, 2 * ATT_Q), jnp.float32),
            pltpu.VMEM((dv + ATT_ONES, 2 * ATT_Q), jnp.float32),
            pltpu.VMEM((2 * ATT_Q, dv), jnp.bfloat16),
            pltpu.VMEM((ATT_K, 2 * ATT_Q), jnp.float32),
            pltpu.VMEM((ATT_K, 2 * ATT_Q), jnp.float32),
            pltpu.VMEM((ATT_K, 2 * ATT_Q), jnp.float32),
        ],
        compiler_params=pltpu.CompilerParams(
            dimension_semantics=("arbitrary", "arbitrary"), vmem_limit_bytes=VMEM_LIMIT),
        name="diff_attn",
    )(q, k, vt, lamv, g_subln)


def _rms(x, g):
    ms = jnp.mean(x * x, axis=-1, keepdims=True)
    return x * lax.rsqrt(ms + RMS_EPS) * g


def _ffn_out_kernel(x_ref, a_ref, o_ref, wout_ref, gffn_ref, wup_ref, wf_ref, bf_ref, wdn_ref,
                    gfin_ref, y_ref, gate_ref, *, final_norm):
    i = pl.program_id(1)
    rows = x_ref.shape[1]
    sub = rows // FFN_SPLIT
    parts = range(FFN_SPLIT)
    n_blocks = FFN_DIM // FFN_BLOCK

    x1 = []
    for s in parts:
        r = slice(s * sub, (s + 1) * sub)
        o_cat = jnp.concatenate([o_ref[0, hd, r, :] for hd in range(N_HEADS)], axis=-1)
        ao = jnp.concatenate([a_ref[0, r, :], o_cat], axis=-1)
        x1.append(x_ref[0, r, :] + _nn_dot(ao, wout_ref[...]))
    h = [_rms(x1[s], gffn_ref[...]).astype(jnp.bfloat16) for s in parts]

    @pl.when(i == 0)
    def _():
        gate_ref[0:FFN_HALO, :] = jnp.zeros((FFN_HALO, FFN_DIM), jnp.float32)

    def up_proj(s, j):
        gate_j = _nn_dot(h[s], wup_ref[:, j * FFN_BLOCK:(j + 1) * FFN_BLOCK])
        val_j = _nn_dot(h[s], wup_ref[:, FFN_DIM + j * FFN_BLOCK:FFN_DIM + (j + 1) * FFN_BLOCK])
        return gate_j, val_j

    x2 = list(x1)
    up_next = [up_proj(s, 0) for s in parts]
    for j in range(n_blocks):
        c0 = j * FFN_BLOCK
        cols = slice(c0, c0 + FFN_BLOCK)
        for s in parts:
            gate_j, val_j = up_next[s]
            if j + 1 < n_blocks:
                up_next[s] = up_proj(s, j + 1)
            base = FFN_HALO + s * sub
            gate_ref[base:base + sub, cols] = gate_j
            conv = bf_ref[:, cols]
            for tap in range(FFN_CONV_KERNEL):
                off = base - (FFN_CONV_KERNEL - 1) + tap
                conv = conv + wf_ref[tap:tap + 1, cols] * gate_ref[off:off + sub, cols]
            hidden = (conv * _sigmoid(conv) * val_j).astype(jnp.bfloat16)
            x2[s] = x2[s] + _nn_dot(hidden, wdn_ref[cols, :])
        carry_rows = gate_ref[rows:rows + FFN_HALO, cols]
        gate_ref[0:FFN_HALO, cols] = carry_rows
    for s in parts:
        out = _rms(x2[s], gfin_ref[...]) if final_norm else x2[s]
        y_ref[0, s * sub:(s + 1) * sub, :] = out


def _ffn_out(x, a, o, w_out, g_ffn, w_up, w_fconv, b_fconv, w_down, g_final, final_norm):
    b, s, d = x.shape
    t = FFN_ROWS
    const = lambda bi, i: (0, 0)
    resident = functools.partial(pl.BlockSpec, index_map=const, pipeline_mode=pl.Buffered(1))
    return pl.pallas_call(
        functools.partial(_ffn_out_kernel, final_norm=final_norm),
        grid=(b, s // t),
        in_specs=[
            pl.BlockSpec((1, t, d), lambda bi, i: (bi, i, 0)),
            pl.BlockSpec((1, t, CONV_WIDTH), lambda bi, i: (bi, i, 0)),
            pl.BlockSpec((1, N_HEADS, t, V_DIM), lambda bi, i: (bi, 0, i, 0)),
            resident((d, d)),
            pl.BlockSpec((1, d), const),
            resident((d, 2 * FFN_DIM)),
            pl.BlockSpec((SUBLANES, FFN_DIM), const),
            pl.BlockSpec((1, FFN_DIM), const),
            resident((FFN_DIM, d)),
            pl.BlockSpec((1, d), const),
        ],
        out_specs=pl.BlockSpec((1, t, d), lambda bi, i: (bi, i, 0)),
        out_shape=jax.ShapeDtypeStruct((b, s, d), jnp.float32),
        scratch_shapes=[pltpu.VMEM((FFN_HALO + t, FFN_DIM), jnp.float32)],
        compiler_params=pltpu.CompilerParams(
            dimension_semantics=("arbitrary", "arbitrary"), vmem_limit_bytes=VMEM_LIMIT),
        name="ffn_out",
    )(x, a, o, w_out, g_ffn, w_up, w_fconv, b_fconv, w_down, g_final)


def _rope_tables(seq):
    inv_freq = 1.0 / (ROPE_THETA ** (jnp.arange(0, HEAD_DIM, 2, dtype=jnp.float32) / HEAD_DIM))
    ang = jnp.arange(seq, dtype=jnp.float32)[:, None] * inv_freq[None, :]
    cos, sin = jnp.cos(ang), jnp.sin(ang)
    reps = LANES // HEAD_DIM
    cos_t = jnp.tile(jnp.concatenate([cos, cos], axis=-1), (1, reps))
    sin_t = jnp.tile(jnp.concatenate([-sin, sin], axis=-1), (1, reps))
    return cos_t, sin_t


def _pad_rows(w, rows):
    return jnp.pad(w, ((0, rows - w.shape[0]), (0, 0)))


def kernel(x, g_mix, w_in, w_dw, b_dw, ln_g, ln_b, lambda_q1, lambda_k1, lambda_q2, lambda_k2,
           g_subln, w_out, g_ffn, w_up, w_fconv, b_fconv, w_down, g_final):
    depth = w_in.shape[0]
    seq = x.shape[1]
    cos_t, sin_t = _rope_tables(seq)
    bf = jnp.bfloat16
    row = lambda v: v.reshape(1, -1).astype(jnp.float32)
    for layer in range(depth):
        w_main = w_in[layer][:, :QK_COLS].astype(bf)
        w_vt = w_in[layer][:, QK_COLS:].T.astype(bf)
        a_pre, q, k, vt = _mix_in(x, row(g_mix[layer]), w_main, w_vt, cos_t, sin_t)
        w_taps = jnp.broadcast_to(w_dw[layer].astype(jnp.float32)[:, None, :],
                                  (CONV_KERNEL, SUBLANES, CONV_WIDTH))
        a = _conv_mod(a_pre, w_taps, row(b_dw[layer]), row(ln_g[layer]), row(ln_b[layer]))
        lam_init = 0.8 - 0.6 * math.exp(-0.3 * layer)
        lam_rows = jnp.stack([lambda_q1[layer], lambda_k1[layer], lambda_q2[layer], lambda_k2[layer]])
        lamv = jnp.pad(lam_rows.astype(jnp.float32),
                       ((0, SUBLANES - 4), (0, LANES - lam_rows.shape[1])))
        o = _diff_attn(q, k, vt, lamv, row(g_subln[layer]), lam_init)
        x = _ffn_out(x, a, o, w_out[layer].astype(bf), row(g_ffn[layer]), w_up[layer].astype(bf),
                     _pad_rows(w_fconv[layer], SUBLANES), row(b_fconv[layer]),
                     w_down[layer].astype(bf), row(g_final), final_norm=(layer == depth - 1))
    return x
```

```python
import functools
import math

import jax
import jax.numpy as jnp
from jax import lax
from jax.experimental import pallas as pl
from jax.experimental.pallas import tpu as pltpu

D_MODEL = 1024
CHUNK = 64
CONV_WIDTH = 512
ATTN_WIDTH = 512
N_HEADS = 4
HEAD_DIM = 64
V_DIM = 2 * HEAD_DIM
CONV_KERNEL = 31
FFN_DIM = 2816
FFN_CONV_KERNEL = 3
ROPE_THETA = 10000.0
RMS_EPS = 1e-6
LN_EPS = 1e-5
SUBLN_EPS = 1e-5
QK_COLS = 2 * CONV_WIDTH + 2 * ATTN_WIDTH

LANES = 128
SUBLANES = 8
LOG2E = math.log2(math.e)
NEG_BIG = -1e30

MIX_ROWS = 512
MIX_SPLIT = 2
CONV_ROWS = 512
CONV_HALO = 32
CONV_CHUNK = 32
ATT_Q = 512
ATT_K = 512
ATT_UNROLL = 8
ATT_ONES = 16
ATT_SAFE = 2.0 ** 60
FFN_ROWS = 512
FFN_SPLIT = 2
FFN_HALO = SUBLANES
FFN_BLOCK = 256

VMEM_LIMIT = 56 * 1024 * 1024


def _nt_dot(a, b):
    return lax.dot_general(a, b, (((1,), (1,)), ((), ())), preferred_element_type=jnp.float32)


def _nn_dot(a, b):
    return jnp.dot(a, b, preferred_element_type=jnp.float32)


def _sigmoid(x):
    return 1.0 / (1.0 + jnp.exp(-x))


def _mix_in_kernel(x_ref, g_ref, w_ref, wvt_ref, cos_ref, sin_ref,
                   apre_ref, q_ref, k_ref, vt_ref, *, q_scale):
    rows = x_ref.shape[1]
    sub = rows // MIX_SPLIT
    lane = lax.broadcasted_iota(jnp.int32, (sub, LANES), 1)
    first_half = (lane % HEAD_DIM) < (HEAD_DIM // 2)
    q0 = 2 * CONV_WIDTH
    k0 = q0 + ATTN_WIDTH
    ones = jnp.ones((ATT_ONES, sub), jnp.bfloat16)

    h = []
    for s in range(MIX_SPLIT):
        x = x_ref[0, s * sub:(s + 1) * sub, :]
        ms = jnp.mean(x * x, axis=-1, keepdims=True)
        h.append((x * lax.rsqrt(ms + RMS_EPS) * g_ref[...]).astype(jnp.bfloat16))
    main = [_nn_dot(h[s], w_ref[...]) for s in range(MIX_SPLIT)]

    for s in range(MIX_SPLIT):
        r = slice(s * sub, (s + 1) * sub)
        c_val = main[s][:, :CONV_WIDTH]
        c_gate = main[s][:, CONV_WIDTH:2 * CONV_WIDTH]
        apre_ref[0, r, :] = c_val * _sigmoid(c_gate)

        cos = cos_ref[r, :]
        sin = sin_ref[r, :]

        def rope(t):
            up = pltpu.roll(t, LANES - HEAD_DIM // 2, axis=1)
            down = pltpu.roll(t, HEAD_DIM // 2, axis=1)
            return t * cos + jnp.where(first_half, up, down) * sin

        for hd in range(N_HEADS):
            qh = main[s][:, q0 + hd * V_DIM:q0 + (hd + 1) * V_DIM]
            kh = main[s][:, k0 + hd * V_DIM:k0 + (hd + 1) * V_DIM]
            q_ref[0, hd, r, :] = (rope(qh) * q_scale).astype(jnp.bfloat16)
            k_ref[0, hd, r, :] = rope(kh).astype(jnp.bfloat16)

    for s in range(MIX_SPLIT):
        vt = _nt_dot(wvt_ref[...], h[s]).astype(jnp.bfloat16)
        j, c0 = divmod(s * sub, ATT_K)
        for hd in range(N_HEADS):
            vt_ref[0, j, hd, 0:V_DIM, c0:c0 + sub] = vt[hd * V_DIM:(hd + 1) * V_DIM, :]
            vt_ref[0, j, hd, V_DIM:V_DIM + ATT_ONES, c0:c0 + sub] = ones


def _mix_in(x, g_mix, w_main, w_vt, cos_t, sin_t):
    b, s, d = x.shape
    t = MIX_ROWS
    q_scale = HEAD_DIM ** -0.5 * LOG2E
    const = lambda bi, i: (0, 0)
    return pl.pallas_call(
        functools.partial(_mix_in_kernel, q_scale=q_scale),
        grid=(b, s // t),
        in_specs=[
            pl.BlockSpec((1, t, d), lambda bi, i: (bi, i, 0)),
            pl.BlockSpec((1, d), const),
            pl.BlockSpec((d, QK_COLS), const),
            pl.BlockSpec((ATTN_WIDTH, d), const),
            pl.BlockSpec((t, LANES), lambda bi, i: (i, 0)),
            pl.BlockSpec((t, LANES), lambda bi, i: (i, 0)),
        ],
        out_specs=[
            pl.BlockSpec((1, t, CONV_WIDTH), lambda bi, i: (bi, i, 0)),
            pl.BlockSpec((1, N_HEADS, t, V_DIM), lambda bi, i: (bi, 0, i, 0)),
            pl.BlockSpec((1, N_HEADS, t, V_DIM), lambda bi, i: (bi, 0, i, 0)),
            pl.BlockSpec((1, t // ATT_K, N_HEADS, V_DIM + ATT_ONES, ATT_K),
                         lambda bi, i: (bi, i, 0, 0, 0)),
        ],
        out_shape=[
            jax.ShapeDtypeStruct((b, s, CONV_WIDTH), jnp.float32),
            jax.ShapeDtypeStruct((b, N_HEADS, s, V_DIM), jnp.bfloat16),
            jax.ShapeDtypeStruct((b, N_HEADS, s, V_DIM), jnp.bfloat16),
            jax.ShapeDtypeStruct((b, s // ATT_K, N_HEADS, V_DIM + ATT_ONES, ATT_K), jnp.bfloat16),
        ],
        compiler_params=pltpu.CompilerParams(
            dimension_semantics=("arbitrary", "arbitrary"), vmem_limit_bytes=VMEM_LIMIT),
        name="mix_in",
    )(x, g_mix, w_main, w_vt, cos_t, sin_t)


def _conv_mod_kernel(cur_ref, halo_ref, w_ref, b_ref, lng_ref, lnb_ref, a_ref, u_ref):
    i = pl.program_id(1)
    rows = cur_ref.shape[1]
    halo = halo_ref[0]
    u_ref[0, 0:CONV_HALO, :] = jnp.where(i == 0, jnp.zeros_like(halo), halo)
    u_ref[0, CONV_HALO:CONV_HALO + rows, :] = cur_ref[0]
    span = CONV_HALO + rows - SUBLANES
    for r in range(1, SUBLANES):
        u_ref[r, 0:span, :] = u_ref[0, r:r + span, :]

    bias = b_ref[...]
    lng = lng_ref[...]
    lnb = lnb_ref[...]
    first_tap = CONV_HALO - (CONV_KERNEL - 1)

    groups = CONV_CHUNK // SUBLANES

    def chunk(c, carry):
        base = c * CONV_CHUNK
        accs = [jnp.broadcast_to(bias, (CONV_CHUNK, CONV_WIDTH)), None]
        for tap in range(CONV_KERNEL):
            off = first_tap + tap
            aligned = base + (off // SUBLANES) * SUBLANES
            w_tap = jnp.concatenate([w_ref[tap]] * groups, axis=0)
            term = w_tap * u_ref[off % SUBLANES, pl.ds(aligned, CONV_CHUNK), :]
            accs[tap % 2] = term if accs[tap % 2] is None else accs[tap % 2] + term
        acc = accs[0] + accs[1]
        mu = jnp.mean(acc, axis=-1, keepdims=True)
        cen = acc - mu
        var = jnp.mean(cen * cen, axis=-1, keepdims=True)
        y = cen * lax.rsqrt(var + LN_EPS) * lng + lnb
        a_ref[0, pl.ds(base, CONV_CHUNK), :] = (y * _sigmoid(y)).astype(a_ref.dtype)
        return carry

    for c in range(rows // CONV_CHUNK):
        chunk(c, 0)


def _conv_mod(a_pre, w_dw, b_dw, ln_g, ln_b):
    b, s, c = a_pre.shape
    t = CONV_ROWS
    per = t // CONV_HALO
    const = lambda bi, i: (0, 0)
    return pl.pallas_call(
        _conv_mod_kernel,
        grid=(b, s // t),
        in_specs=[
            pl.BlockSpec((1, t, c), lambda bi, i: (bi, i, 0)),
            pl.BlockSpec((1, CONV_HALO, c), lambda bi, i: (bi, jnp.maximum(i * per - 1, 0), 0)),
            pl.BlockSpec((CONV_KERNEL, SUBLANES, c), lambda bi, i: (0, 0, 0)),
            pl.BlockSpec((1, c), const),
            pl.BlockSpec((1, c), const),
            pl.BlockSpec((1, c), const),
        ],
        out_specs=pl.BlockSpec((1, t, c), lambda bi, i: (bi, i, 0)),
        out_shape=jax.ShapeDtypeStruct((b, s, c), jnp.bfloat16),
        scratch_shapes=[pltpu.VMEM((SUBLANES, CONV_HALO + t, c), jnp.float32)],
        compiler_params=pltpu.CompilerParams(
            dimension_semantics=("arbitrary", "arbitrary"), vmem_limit_bytes=VMEM_LIMIT),
        name="conv_mod",
    )(a_pre, a_pre, w_dw, b_dw, ln_g, ln_b)


def _diff_attn_kernel(q_ref, k_ref, vt_ref, lamv_ref, g_ref, o_ref,
                      m_ref, acc_ref, qq_ref, sa_ref, sb_ref, mask_ref, *, lam_init):
    seq = q_ref.shape[2]
    tq, tk = ATT_Q, ATT_K
    lamv = lamv_ref[...]
    lam = (jnp.exp(jnp.sum(lamv[0:1] * lamv[1:2], axis=-1, keepdims=True))
           - jnp.exp(jnp.sum(lamv[2:3] * lamv[3:4], axis=-1, keepdims=True)) + lam_init)
    gain = g_ref[...] * (1.0 - lam_init)

    lane = lax.broadcasted_iota(jnp.int32, (tq, V_DIM), 1)
    key_chunk = lax.broadcasted_iota(jnp.int32, (tk, 2 * tq), 0) // CHUNK
    qry_chunk = (lax.broadcasted_iota(jnp.int32, (tk, 2 * tq), 1) % tq) // CHUNK
    @pl.when(jnp.logical_and(pl.program_id(0) == 0, pl.program_id(1) == 0))
    def _():
        mask_ref[...] = jnp.where(key_chunk <= qry_chunk, 0.0, NEG_BIG)

    def scores(ki, s_ref):
        k0 = pl.multiple_of(ki * tk, tk)
        s_ref[...] = _nt_dot(k_ref[0, 0, pl.ds(k0, tk), :], qq_ref[...])

    def values(ki, p):
        return _nn_dot(vt_ref[0, ki, 0], p.astype(jnp.bfloat16))

    def anchor(qi, s_ref):
        m_ref[...] = jnp.max(s_ref[0:CHUNK, :], axis=0, keepdims=True)
        acc_ref[...] = values(qi, jnp.exp2(s_ref[...] + mask_ref[...] - m_ref[...]))

    def add_tile(ki, s_ref):
        acc_ref[...] += values(ki, jnp.exp2(s_ref[...] - m_ref[...]))

    def online_tile(ki, s_ref, masked):
        s = s_ref[...]
        if masked:
            s = s + mask_ref[...]
        m_old = m_ref[...]
        m_new = jnp.maximum(m_old, jnp.max(s, axis=0, keepdims=True))
        acc_ref[...] = jnp.exp2(m_old - m_new) * acc_ref[...] + values(ki, jnp.exp2(s - m_new))
        m_ref[...] = m_new

    def settle(tile):
        denom = acc_ref[V_DIM:V_DIM + 1, :]
        in_range = jnp.logical_and(denom < ATT_SAFE, denom > 1.0 / ATT_SAFE)
        overflowed = jnp.max(jnp.where(in_range, 0.0, 1.0)) > 0.5

        @pl.when(overflowed)
        def _():
            m_ref[...] = jnp.full(m_ref.shape, NEG_BIG, jnp.float32)
            acc_ref[...] = jnp.zeros(acc_ref.shape, jnp.float32)

            def step(ki, c):
                scores(ki, sa_ref)
                online_tile(ki, sa_ref, False)
                return c

            lax.fori_loop(0, tile, step, 0)
            scores(tile, sa_ref)
            online_tile(tile, sa_ref, True)

    def emit(tile):
        acc = acc_ref[...]
        inv_l = 1.0 / acc[V_DIM:V_DIM + 1, :]
        o_t = (acc[:V_DIM, :tq] * inv_l[:, :tq]
               - lam * (acc[:V_DIM, tq:] * inv_l[:, tq:]))
        o = o_t.T
        ms = jnp.mean(o * o, axis=-1, keepdims=True)
        r0 = pl.multiple_of(tile * tq, tq)
        o_ref[0, 0, pl.ds(r0, tq), :] = (o * lax.rsqrt(ms + SUBLN_EPS) * gain).astype(o_ref.dtype)

    def q_tile(qi, carry):
        prev = jnp.maximum(qi - 1, 0)
        settle(prev)
        emit(prev)

        q0 = pl.multiple_of(qi * tq, tq)
        qt = q_ref[0, 0, pl.ds(q0, tq), :]
        zero = jnp.zeros_like(qt)
        qq_ref[0:tq, :] = jnp.where(lane < HEAD_DIM, qt, zero)
        qq_ref[tq:2 * tq, :] = jnp.where(lane >= HEAD_DIM, qt, zero)

        scores(qi, sb_ref)
        scores(0, sa_ref)
        anchor(qi, sb_ref)

        bufs = (sa_ref, sb_ref)

        def run(first, count, lookahead):
            for j in range(count):
                if j + 1 < count or lookahead:
                    scores(first + j + 1, bufs[(j + 1) % 2])
                add_tile(first + j, bufs[j % 2])

        n_groups = (qi - 1) // ATT_UNROLL

        def group(i, c):
            run(ATT_UNROLL * i, ATT_UNROLL, True)
            return c

        lax.fori_loop(0, n_groups, group, 0)

        left = qi - ATT_UNROLL * jnp.maximum(n_groups, 0)
        for r in range(1, ATT_UNROLL + 1):
            @pl.when(left == r)
            def _(r=r):
                run(qi - r, r, False)

        return carry

    n_q = seq // tq
    acc_ref[...] = jnp.ones(acc_ref.shape, jnp.float32)
    lax.fori_loop(0, n_q, q_tile, 0)
    settle(n_q - 1)
    emit(n_q - 1)


def _diff_attn(q, k, vt, lamv, g_subln, lam_init):
    b, nh, s, dv = q.shape
    return pl.pallas_call(
        functools.partial(_diff_attn_kernel, lam_init=lam_init),
        grid=(b, nh),
        in_specs=[
            pl.BlockSpec((1, 1, s, dv), lambda bi, hi: (bi, hi, 0, 0)),
            pl.BlockSpec((1, 1, s, dv), lambda bi, hi: (bi, hi, 0, 0)),
            pl.BlockSpec((1, s // ATT_K, 1, dv + ATT_ONES, ATT_K), lambda bi, hi: (bi, 0, hi, 0, 0)),
            pl.BlockSpec((SUBLANES, LANES), lambda bi, hi: (0, 0)),
            pl.BlockSpec((1, dv), lambda bi, hi: (0, 0)),
        ],
        out_specs=pl.BlockSpec((1, 1, s, dv), lambda bi, hi: (bi, hi, 0, 0)),
        out_shape=jax.ShapeDtypeStruct((b, nh, s, dv), jnp.bfloat16),
        scratch_shapes=[
            pltpu.VMEM((1, 2 * ATT_Q), jnp.float32),
            pltpu.VMEM((dv + ATT_ONES, 2 * ATT_Q), jnp.float32),
            pltpu.VMEM((2 * ATT_Q, dv), jnp.bfloat16),
            pltpu.VMEM((ATT_K, 2 * ATT_Q), jnp.float32),
            pltpu.VMEM((ATT_K, 2 * ATT_Q), jnp.float32),
            pltpu.VMEM((ATT_K, 2 * ATT_Q), jnp.float32),
        ],
        compiler_params=pltpu.CompilerParams(
            dimension_semantics=("arbitrary", "arbitrary"), vmem_limit_bytes=VMEM_LIMIT),
        name="diff_attn",
    )(q, k, vt, lamv, g_subln)


def _rms(x, g):
    ms = jnp.mean(x * x, axis=-1, keepdims=True)
    return x * lax.rsqrt(ms + RMS_EPS) * g


def _ffn_out_kernel(x_ref, a_ref, o_ref, wout_ref, gffn_ref, wup_ref, wf_ref, bf_ref, wdn_ref,
                    gfin_ref, y_ref, gate_ref, *, final_norm):
    i = pl.program_id(1)
    rows = x_ref.shape[1]
    sub = rows // FFN_SPLIT
    parts = range(FFN_SPLIT)
    n_blocks = FFN_DIM // FFN_BLOCK

    x1 = []
    for s in parts:
        r = slice(s * sub, (s + 1) * sub)
        o_cat = jnp.concatenate([o_ref[0, hd, r, :] for hd in range(N_HEADS)], axis=-1)
        ao = jnp.concatenate([a_ref[0, r, :], o_cat], axis=-1)
        x1.append(x_ref[0, r, :] + _nn_dot(ao, wout_ref[...]))
    h = [_rms(x1[s], gffn_ref[...]).astype(jnp.bfloat16) for s in parts]

    @pl.when(i == 0)
    def _():
        gate_ref[0:FFN_HALO, :] = jnp.zeros((FFN_HALO, FFN_DIM), jnp.float32)

    def up_proj(s, j):
        gate_j = _nn_dot(h[s], wup_ref[:, j * FFN_BLOCK:(j + 1) * FFN_BLOCK])
        val_j = _nn_dot(h[s], wup_ref[:, FFN_DIM + j * FFN_BLOCK:FFN_DIM + (j + 1) * FFN_BLOCK])
        return gate_j, val_j

    x2 = list(x1)
    up_next = [up_proj(s, 0) for s in parts]
    for j in range(n_blocks):
        c0 = j * FFN_BLOCK
        cols = slice(c0, c0 + FFN_BLOCK)
        for s in parts:
            gate_j, val_j = up_next[s]
            if j + 1 < n_blocks:
                up_next[s] = up_proj(s, j + 1)
            base = FFN_HALO + s * sub
            gate_ref[base:base + sub, cols] = gate_j
            conv = bf_ref[:, cols]
            for tap in range(FFN_CONV_KERNEL):
                off = base - (FFN_CONV_KERNEL - 1) + tap
                conv = conv + wf_ref[tap:tap + 1, cols] * gate_ref[off:off + sub, cols]
            hidden = (conv * _sigmoid(conv) * val_j).astype(jnp.bfloat16)
            x2[s] = x2[s] + _nn_dot(hidden, wdn_ref[cols, :])
        carry_rows = gate_ref[rows:rows + FFN_HALO, cols]
        gate_ref[0:FFN_HALO, cols] = carry_rows
    for s in parts:
        out = _rms(x2[s], gfin_ref[...]) if final_norm else x2[s]
        y_ref[0, s * sub:(s + 1) * sub, :] = out


def _ffn_out(x, a, o, w_out, g_ffn, w_up, w_fconv, b_fconv, w_down, g_final, final_norm):
    b, s, d = x.shape
    t = FFN_ROWS
    const = lambda bi, i: (0, 0)
    resident = functools.partial(pl.BlockSpec, index_map=const, pipeline_mode=pl.Buffered(1))
    return pl.pallas_call(
        functools.partial(_ffn_out_kernel, final_norm=final_norm),
        grid=(b, s // t),
        in_specs=[
            pl.BlockSpec((1, t, d), lambda bi, i: (bi, i, 0)),
            pl.BlockSpec((1, t, CONV_WIDTH), lambda bi, i: (bi, i, 0)),
            pl.BlockSpec((1, N_HEADS, t, V_DIM), lambda bi, i: (bi, 0, i, 0)),
            resident((d, d)),
            pl.BlockSpec((1, d), const),
            resident((d, 2 * FFN_DIM)),
            pl.BlockSpec((SUBLANES, FFN_DIM), const),
            pl.BlockSpec((1, FFN_DIM), const),
            resident((FFN_DIM, d)),
            pl.BlockSpec((1, d), const),
        ],
        out_specs=pl.BlockSpec((1, t, d), lambda bi, i: (bi, i, 0)),
        out_shape=jax.ShapeDtypeStruct((b, s, d), jnp.float32),
        scratch_shapes=[pltpu.VMEM((FFN_HALO + t, FFN_DIM), jnp.float32)],
        compiler_params=pltpu.CompilerParams(
            dimension_semantics=("arbitrary", "arbitrary"), vmem_limit_bytes=VMEM_LIMIT),
        name="ffn_out",
    )(x, a, o, w_out, g_ffn, w_up, w_fconv, b_fconv, w_down, g_final)


def _rope_tables(seq):
    inv_freq = 1.0 / (ROPE_THETA ** (jnp.arange(0, HEAD_DIM, 2, dtype=jnp.float32) / HEAD_DIM))
    ang = jnp.arange(seq, dtype=jnp.float32)[:, None] * inv_freq[None, :]
    cos, sin = jnp.cos(ang), jnp.sin(ang)
    reps = LANES // HEAD_DIM
    cos_t = jnp.tile(jnp.concatenate([cos, cos], axis=-1), (1, reps))
    sin_t = jnp.tile(jnp.concatenate([-sin, sin], axis=-1), (1, reps))
    return cos_t, sin_t


def _pad_rows(w, rows):
    return jnp.pad(w, ((0, rows - w.shape[0]), (0, 0)))


def kernel(x, g_mix, w_in, w_dw, b_dw, ln_g, ln_b, lambda_q1, lambda_k1, lambda_q2, lambda_k2,
           g_subln, w_out, g_ffn, w_up, w_fconv, b_fconv, w_down, g_final):
    depth = w_in.shape[0]
    seq = x.shape[1]
    cos_t, sin_t = _rope_tables(seq)
    bf = jnp.bfloat16
    row = lambda v: v.reshape(1, -1).astype(jnp.float32)
    for layer in range(depth):
        w_main = w_in[layer][:, :QK_COLS].astype(bf)
        w_vt = w_in[layer][:, QK_COLS:].T.astype(bf)
        a_pre, q, k, vt = _mix_in(x, row(g_mix[layer]), w_main, w_vt, cos_t, sin_t)
        w_taps = jnp.broadcast_to(w_dw[layer].astype(jnp.float32)[:, None, :],
                                  (CONV_KERNEL, SUBLANES, CONV_WIDTH))
        a = _conv_mod(a_pre, w_taps, row(b_dw[layer]), row(ln_g[layer]), row(ln_b[layer]))
        lam_init = 0.8 - 0.6 * math.exp(-0.3 * layer)
        lam_rows = jnp.stack([lambda_q1[layer], lambda_k1[layer], lambda_q2[layer], lambda_k2[layer]])
        lamv = jnp.pad(lam_rows.astype(jnp.float32),
                       ((0, SUBLANES - 4), (0, LANES - lam_rows.shape[1])))
        o = _diff_attn(q, k, vt, lamv, row(g_subln[layer]), lam_init)
        x = _ffn_out(x, a, o, w_out[layer].astype(bf), row(g_ffn[layer]), w_up[layer].astype(bf),
                     _pad_rows(w_fconv[layer], SUBLANES), row(b_fconv[layer]),
                     w_down[layer].astype(bf), row(g_final), final_norm=(layer == depth - 1))
    return x
```

```python
import functools
import math

import jax
import jax.numpy as jnp
from jax import lax
from jax.experimental import pallas as pl
from jax.experimental.pallas import tpu as pltpu

D_MODEL = 1024
CHUNK = 64
CONV_WIDTH = 512
ATTN_WIDTH = 512
N_HEADS = 4
HEAD_DIM = 64
V_DIM = 2 * HEAD_DIM
CONV_KERNEL = 31
FFN_DIM = 2816
FFN_CONV_KERNEL = 3
ROPE_THETA = 10000.0
RMS_EPS = 1e-6
LN_EPS = 1e-5
SUBLN_EPS = 1e-5
QK_COLS = 2 * CONV_WIDTH + 2 * ATTN_WIDTH

LANES = 128
SUBLANES = 8
LOG2E = math.log2(math.e)
NEG_BIG = -1e30

MIX_ROWS = 512
MIX_SPLIT = 2
CONV_ROWS = 512
CONV_HALO = 32
CONV_CHUNK = 32
ATT_Q = 512
ATT_K = 512
ATT_UNROLL = 10
ATT_ONES = 16
ATT_SAFE = 2.0 ** 60
FFN_ROWS = 512
FFN_SPLIT = 2
FFN_HALO = SUBLANES
FFN_BLOCK = 256

VMEM_LIMIT = 56 * 1024 * 1024


def _nt_dot(a, b):
    return lax.dot_general(a, b, (((1,), (1,)), ((), ())), preferred_element_type=jnp.float32)


def _nn_dot(a, b):
    return jnp.dot(a, b, preferred_element_type=jnp.float32)


def _sigmoid(x):
    return 1.0 / (1.0 + jnp.exp(-x))


def _mix_in_kernel(x_ref, g_ref, w_ref, wvt_ref, cos_ref, sin_ref,
                   apre_ref, q_ref, k_ref, vt_ref, *, q_scale):
    rows = x_ref.shape[1]
    sub = rows // MIX_SPLIT
    lane = lax.broadcasted_iota(jnp.int32, (sub, LANES), 1)
    first_half = (lane % HEAD_DIM) < (HEAD_DIM // 2)
    q0 = 2 * CONV_WIDTH
    k0 = q0 + ATTN_WIDTH
    ones = jnp.ones((ATT_ONES, sub), jnp.bfloat16)

    h = []
    for s in range(MIX_SPLIT):
        x = x_ref[0, s * sub:(s + 1) * sub, :]
        ms = jnp.mean(x * x, axis=-1, keepdims=True)
        h.append((x * lax.rsqrt(ms + RMS_EPS) * g_ref[...]).astype(jnp.bfloat16))
    main = [_nn_dot(h[s], w_ref[...]) for s in range(MIX_SPLIT)]

    for s in range(MIX_SPLIT):
        r = slice(s * sub, (s + 1) * sub)
        c_val = main[s][:, :CONV_WIDTH]
        c_gate = main[s][:, CONV_WIDTH:2 * CONV_WIDTH]
        apre_ref[0, r, :] = c_val * _sigmoid(c_gate)

        cos = cos_ref[r, :]
        sin = sin_ref[r, :]

        def rope(t):
            up = pltpu.roll(t, LANES - HEAD_DIM // 2, axis=1)
            down = pltpu.roll(t, HEAD_DIM // 2, axis=1)
            return t * cos + jnp.where(first_half, up, down) * sin

        for hd in range(N_HEADS):
            qh = main[s][:, q0 + hd * V_DIM:q0 + (hd + 1) * V_DIM]
            kh = main[s][:, k0 + hd * V_DIM:k0 + (hd + 1) * V_DIM]
            q_ref[0, hd, r, :] = (rope(qh) * q_scale).astype(jnp.bfloat16)
            k_ref[0, hd, r, :] = rope(kh).astype(jnp.bfloat16)

    for s in range(MIX_SPLIT):
        vt = _nt_dot(wvt_ref[...], h[s]).astype(jnp.bfloat16)
        j, c0 = divmod(s * sub, ATT_K)
        for hd in range(N_HEADS):
            vt_ref[0, j, hd, 0:V_DIM, c0:c0 + sub] = vt[hd * V_DIM:(hd + 1) * V_DIM, :]
            vt_ref[0, j, hd, V_DIM:V_DIM + ATT_ONES, c0:c0 + sub] = ones


def _mix_in(x, g_mix, w_main, w_vt, cos_t, sin_t):
    b, s, d = x.shape
    t = MIX_ROWS
    q_scale = HEAD_DIM ** -0.5 * LOG2E
    const = lambda bi, i: (0, 0)
    return pl.pallas_call(
        functools.partial(_mix_in_kernel, q_scale=q_scale),
        grid=(b, s // t),
        in_specs=[
            pl.BlockSpec((1, t, d), lambda bi, i: (bi, i, 0)),
            pl.BlockSpec((1, d), const),
            pl.BlockSpec((d, QK_COLS), const),
            pl.BlockSpec((ATTN_WIDTH, d), const),
            pl.BlockSpec((t, LANES), lambda bi, i: (i, 0)),
            pl.BlockSpec((t, LANES), lambda bi, i: (i, 0)),
        ],
        out_specs=[
            pl.BlockSpec((1, t, CONV_WIDTH), lambda bi, i: (bi, i, 0)),
            pl.BlockSpec((1, N_HEADS, t, V_DIM), lambda bi, i: (bi, 0, i, 0)),
            pl.BlockSpec((1, N_HEADS, t, V_DIM), lambda bi, i: (bi, 0, i, 0)),
            pl.BlockSpec((1, t // ATT_K, N_HEADS, V_DIM + ATT_ONES, ATT_K),
                         lambda bi, i: (bi, i, 0, 0, 0)),
        ],
        out_shape=[
            jax.ShapeDtypeStruct((b, s, CONV_WIDTH), jnp.float32),
            jax.ShapeDtypeStruct((b, N_HEADS, s, V_DIM), jnp.bfloat16),
            jax.ShapeDtypeStruct((b, N_HEADS, s, V_DIM), jnp.bfloat16),
            jax.ShapeDtypeStruct((b, s // ATT_K, N_HEADS, V_DIM + ATT_ONES, ATT_K), jnp.bfloat16),
        ],
        compiler_params=pltpu.CompilerParams(
            dimension_semantics=("arbitrary", "arbitrary"), vmem_limit_bytes=VMEM_LIMIT),
        name="mix_in",
    )(x, g_mix, w_main, w_vt, cos_t, sin_t)


def _conv_mod_kernel(cur_ref, halo_ref, w_ref, b_ref, lng_ref, lnb_ref, a_ref, u_ref):
    i = pl.program_id(1)
    rows = cur_ref.shape[1]
    halo = halo_ref[0]
    u_ref[0, 0:CONV_HALO, :] = jnp.where(i == 0, jnp.zeros_like(halo), halo)
    u_ref[0, CONV_HALO:CONV_HALO + rows, :] = cur_ref[0]
    span = CONV_HALO + rows - SUBLANES
    for r in range(1, SUBLANES):
        u_ref[r, 0:span, :] = u_ref[0, r:r + span, :]

    bias = b_ref[...]
    lng = lng_ref[...]
    lnb = lnb_ref[...]
    first_tap = CONV_HALO - (CONV_KERNEL - 1)

    groups = CONV_CHUNK // SUBLANES

    def chunk(c, carry):
        base = c * CONV_CHUNK
        accs = [jnp.broadcast_to(bias, (CONV_CHUNK, CONV_WIDTH)), None]
        for tap in range(CONV_KERNEL):
            off = first_tap + tap
            aligned = base + (off // SUBLANES) * SUBLANES
            w_tap = jnp.concatenate([w_ref[tap]] * groups, axis=0)
            term = w_tap * u_ref[off % SUBLANES, pl.ds(aligned, CONV_CHUNK), :]
            accs[tap % 2] = term if accs[tap % 2] is None else accs[tap % 2] + term
        acc = accs[0] + accs[1]
        mu = jnp.mean(acc, axis=-1, keepdims=True)
        cen = acc - mu
        var = jnp.mean(cen * cen, axis=-1, keepdims=True)
        y = cen * lax.rsqrt(var + LN_EPS) * lng + lnb
        a_ref[0, pl.ds(base, CONV_CHUNK), :] = (y * _sigmoid(y)).astype(a_ref.dtype)
        return carry

    for c in range(rows // CONV_CHUNK):
        chunk(c, 0)


def _conv_mod(a_pre, w_dw, b_dw, ln_g, ln_b):
    b, s, c = a_pre.shape
    t = CONV_ROWS
    per = t // CONV_HALO
    const = lambda bi, i: (0, 0)
    return pl.pallas_call(
        _conv_mod_kernel,
        grid=(b, s // t),
        in_specs=[
            pl.BlockSpec((1, t, c), lambda bi, i: (bi, i, 0)),
            pl.BlockSpec((1, CONV_HALO, c), lambda bi, i: (bi, jnp.maximum(i * per - 1, 0), 0)),
            pl.BlockSpec((CONV_KERNEL, SUBLANES, c), lambda bi, i: (0, 0, 0)),
            pl.BlockSpec((1, c), const),
            pl.BlockSpec((1, c), const),
            pl.BlockSpec((1, c), const),
        ],
        out_specs=pl.BlockSpec((1, t, c), lambda bi, i: (bi, i, 0)),
        out_shape=jax.ShapeDtypeStruct((b, s, c), jnp.bfloat16),
        scratch_shapes=[pltpu.VMEM((SUBLANES, CONV_HALO + t, c), jnp.float32)],
        compiler_params=pltpu.CompilerParams(
            dimension_semantics=("arbitrary", "arbitrary"), vmem_limit_bytes=VMEM_LIMIT),
        name="conv_mod",
    )(a_pre, a_pre, w_dw, b_dw, ln_g, ln_b)


def _diff_attn_kernel(q_ref, k_ref, vt_ref, lamv_ref, g_ref, o_ref,
                      m_ref, acc_ref, qq_ref, sa_ref, sb_ref, mask_ref, *, lam_init):
    seq = q_ref.shape[2]
    tq, tk = ATT_Q, ATT_K
    lamv = lamv_ref[...]
    lam = (jnp.exp(jnp.sum(lamv[0:1] * lamv[1:2], axis=-1, keepdims=True))
           - jnp.exp(jnp.sum(lamv[2:3] * lamv[3:4], axis=-1, keepdims=True)) + lam_init)
    gain = g_ref[...] * (1.0 - lam_init)

    lane = lax.broadcasted_iota(jnp.int32, (tq, V_DIM), 1)
    key_chunk = lax.broadcasted_iota(jnp.int32, (tk, 2 * tq), 0) // CHUNK
    qry_chunk = (lax.broadcasted_iota(jnp.int32, (tk, 2 * tq), 1) % tq) // CHUNK
    @pl.when(jnp.logical_and(pl.program_id(0) == 0, pl.program_id(1) == 0))
    def _():
        mask_ref[...] = jnp.where(key_chunk <= qry_chunk, 0.0, NEG_BIG)

    def scores(ki, s_ref):
        k0 = pl.multiple_of(ki * tk, tk)
        s_ref[...] = _nt_dot(k_ref[0, 0, pl.ds(k0, tk), :], qq_ref[...])

    def values(ki, p):
        return _nn_dot(vt_ref[0, ki, 0], p.astype(jnp.bfloat16))

    def anchor(qi, s_ref):
        m_ref[...] = jnp.max(s_ref[0:CHUNK, :], axis=0, keepdims=True)
        acc_ref[...] = values(qi, jnp.exp2(s_ref[...] + mask_ref[...] - m_ref[...]))

    def add_tile(ki, s_ref):
        acc_ref[...] += values(ki, jnp.exp2(s_ref[...] - m_ref[...]))

    def online_tile(ki, s_ref, masked):
        s = s_ref[...]
        if masked:
            s = s + mask_ref[...]
        m_old = m_ref[...]
        m_new = jnp.maximum(m_old, jnp.max(s, axis=0, keepdims=True))
        acc_ref[...] = jnp.exp2(m_old - m_new) * acc_ref[...] + values(ki, jnp.exp2(s - m_new))
        m_ref[...] = m_new

    def settle(tile):
        denom = acc_ref[V_DIM:V_DIM + 1, :]
        in_range = jnp.logical_and(denom < ATT_SAFE, denom > 1.0 / ATT_SAFE)
        overflowed = jnp.max(jnp.where(in_range, 0.0, 1.0)) > 0.5

        @pl.when(overflowed)
        def _():
            m_ref[...] = jnp.full(m_ref.shape, NEG_BIG, jnp.float32)
            acc_ref[...] = jnp.zeros(acc_ref.shape, jnp.float32)

            def step(ki, c):
                scores(ki, sa_ref)
                online_tile(ki, sa_ref, False)
                return c

            lax.fori_loop(0, tile, step, 0)
            scores(tile, sa_ref)
            online_tile(tile, sa_ref, True)

    def emit(tile):
        acc = acc_ref[...]
        inv_l = 1.0 / acc[V_DIM:V_DIM + 1, :]
        o_t = (acc[:V_DIM, :tq] * inv_l[:, :tq]
               - lam * (acc[:V_DIM, tq:] * inv_l[:, tq:]))
        o = o_t.T
        ms = jnp.mean(o * o, axis=-1, keepdims=True)
        r0 = pl.multiple_of(tile * tq, tq)
        o_ref[0, 0, pl.ds(r0, tq), :] = (o * lax.rsqrt(ms + SUBLN_EPS) * gain).astype(o_ref.dtype)

    def q_tile(qi, carry):
        prev = jnp.maximum(qi - 1, 0)
        settle(prev)
        emit(prev)

        q0 = pl.multiple_of(qi * tq, tq)
        qt = q_ref[0, 0, pl.ds(q0, tq), :]
        zero = jnp.zeros_like(qt)
        qq_ref[0:tq, :] = jnp.where(lane < HEAD_DIM, qt, zero)
        qq_ref[tq:2 * tq, :] = jnp.where(lane >= HEAD_DIM, qt, zero)

        scores(qi, sb_ref)
        scores(0, sa_ref)
        anchor(qi, sb_ref)

        bufs = (sa_ref, sb_ref)

        def run(first, count, lookahead):
            for j in range(count):
                if j + 1 < count or lookahead:
                    scores(first + j + 1, bufs[(j + 1) % 2])
                add_tile(first + j, bufs[j % 2])

        n_groups = (qi - 1) // ATT_UNROLL

        def group(i, c):
            run(ATT_UNROLL * i, ATT_UNROLL, True)
            return c

        lax.fori_loop(0, n_groups, group, 0)

        left = qi - ATT_UNROLL * jnp.maximum(n_groups, 0)
        for r in range(1, ATT_UNROLL + 1):
            @pl.when(left == r)
            def _(r=r):
                run(qi - r, r, False)

        return carry

    n_q = seq // tq
    acc_ref[...] = jnp.ones(acc_ref.shape, jnp.float32)
    lax.fori_loop(0, n_q, q_tile, 0)
    settle(n_q - 1)
    emit(n_q - 1)


def _diff_attn(q, k, vt, lamv, g_subln, lam_init):
    b, nh, s, dv = q.shape
    return pl.pallas_call(
        functools.partial(_diff_attn_kernel, lam_init=lam_init),
        grid=(b, nh),
        in_specs=[
            pl.BlockSpec((1, 1, s, dv), lambda bi, hi: (bi, hi, 0, 0)),
            pl.BlockSpec((1, 1, s, dv), lambda bi, hi: (bi, hi, 0, 0)),
            pl.BlockSpec((1, s // ATT_K, 1, dv + ATT_ONES, ATT_K), lambda bi, hi: (bi, 0, hi, 0, 0)),
            pl.BlockSpec((SUBLANES, LANES), lambda bi, hi: (0, 0)),
            pl.BlockSpec((1, dv), lambda bi, hi: (0, 0)),
        ],
        out_specs=pl.BlockSpec((1, 1, s, dv), lambda bi, hi: (bi, hi, 0, 0)),
        out_shape=jax.ShapeDtypeStruct((b, nh, s, dv), jnp.bfloat16),
        scratch_shapes=[
            pltpu.VMEM((1, 2 * ATT_Q), jnp.float32),
            pltpu.VMEM((dv + ATT_ONES, 2 * ATT_Q), jnp.float32),
            pltpu.VMEM((2 * ATT_Q, dv), jnp.bfloat16),
            pltpu.VMEM((ATT_K, 2 * ATT_Q), jnp.float32),
            pltpu.VMEM((ATT_K, 2 * ATT_Q), jnp.float32),
            pltpu.VMEM((ATT_K, 2 * ATT_Q), jnp.float32),
        ],
        compiler_params=pltpu.CompilerParams(
            dimension_semantics=("arbitrary", "arbitrary"), vmem_limit_bytes=VMEM_LIMIT),
        name="diff_attn",
    )(q, k, vt, lamv, g_subln)


def _rms(x, g):
    ms = jnp.mean(x * x, axis=-1, keepdims=True)
    return x * lax.rsqrt(ms + RMS_EPS) * g


def _ffn_out_kernel(x_ref, a_ref, o_ref, wout_ref, gffn_ref, wup_ref, wf_ref, bf_ref, wdn_ref,
                    gfin_ref, y_ref, gate_ref, *, final_norm):
    i = pl.program_id(1)
    rows = x_ref.shape[1]
    sub = rows // FFN_SPLIT
    parts = range(FFN_SPLIT)
    n_blocks = FFN_DIM // FFN_BLOCK

    x1 = []
    for s in parts:
        r = slice(s * sub, (s + 1) * sub)
        o_cat = jnp.concatenate([o_ref[0, hd, r, :] for hd in range(N_HEADS)], axis=-1)
        ao = jnp.concatenate([a_ref[0, r, :], o_cat], axis=-1)
        x1.append(x_ref[0, r, :] + _nn_dot(ao, wout_ref[...]))
    h = [_rms(x1[s], gffn_ref[...]).astype(jnp.bfloat16) for s in parts]

    @pl.when(i == 0)
    def _():
        gate_ref[0:FFN_HALO, :] = jnp.zeros((FFN_HALO, FFN_DIM), jnp.float32)

    def up_proj(s, j):
        gate_j = _nn_dot(h[s], wup_ref[:, j * FFN_BLOCK:(j + 1) * FFN_BLOCK])
        val_j = _nn_dot(h[s], wup_ref[:, FFN_DIM + j * FFN_BLOCK:FFN_DIM + (j + 1) * FFN_BLOCK])
        return gate_j, val_j

    x2 = list(x1)
    up_next = [up_proj(s, 0) for s in parts]
    for j in range(n_blocks):
        c0 = j * FFN_BLOCK
        cols = slice(c0, c0 + FFN_BLOCK)
        for s in parts:
            gate_j, val_j = up_next[s]
            if j + 1 < n_blocks:
                up_next[s] = up_proj(s, j + 1)
            base = FFN_HALO + s * sub
            gate_ref[base:base + sub, cols] = gate_j
            conv = bf_ref[:, cols]
            for tap in range(FFN_CONV_KERNEL):
                off = base - (FFN_CONV_KERNEL - 1) + tap
                conv = conv + wf_ref[tap:tap + 1, cols] * gate_ref[off:off + sub, cols]
            hidden = (conv * _sigmoid(conv) * val_j).astype(jnp.bfloat16)
            x2[s] = x2[s] + _nn_dot(hidden, wdn_ref[cols, :])
        carry_rows = gate_ref[rows:rows + FFN_HALO, cols]
        gate_ref[0:FFN_HALO, cols] = carry_rows
    for s in parts:
        out = _rms(x2[s], gfin_ref[...]) if final_norm else x2[s]
        y_ref[0, s * sub:(s + 1) * sub, :] = out


def _ffn_out(x, a, o, w_out, g_ffn, w_up, w_fconv, b_fconv, w_down, g_final, final_norm):
    b, s, d = x.shape
    t = FFN_ROWS
    const = lambda bi, i: (0, 0)
    resident = functools.partial(pl.BlockSpec, index_map=const, pipeline_mode=pl.Buffered(1))
    return pl.pallas_call(
        functools.partial(_ffn_out_kernel, final_norm=final_norm),
        grid=(b, s // t),
        in_specs=[
            pl.BlockSpec((1, t, d), lambda bi, i: (bi, i, 0)),
            pl.BlockSpec((1, t, CONV_WIDTH), lambda bi, i: (bi, i, 0)),
            pl.BlockSpec((1, N_HEADS, t, V_DIM), lambda bi, i: (bi, 0, i, 0)),
            resident((d, d)),
            pl.BlockSpec((1, d), const),
            resident((d, 2 * FFN_DIM)),
            pl.BlockSpec((SUBLANES, FFN_DIM), const),
            pl.BlockSpec((1, FFN_DIM), const),
            resident((FFN_DIM, d)),
            pl.BlockSpec((1, d), const),
        ],
        out_specs=pl.BlockSpec((1, t, d), lambda bi, i: (bi, i, 0)),
        out_shape=jax.ShapeDtypeStruct((b, s, d), jnp.float32),
        scratch_shapes=[pltpu.VMEM((FFN_HALO + t, FFN_DIM), jnp.float32)],
        compiler_params=pltpu.CompilerParams(
            dimension_semantics=("arbitrary", "arbitrary"), vmem_limit_bytes=VMEM_LIMIT),
        name="ffn_out",
    )(x, a, o, w_out, g_ffn, w_up, w_fconv, b_fconv, w_down, g_final)


def _rope_tables(seq):
    inv_freq = 1.0 / (ROPE_THETA ** (jnp.arange(0, HEAD_DIM, 2, dtype=jnp.float32) / HEAD_DIM))
    ang = jnp.arange(seq, dtype=jnp.float32)[:, None] * inv_freq[None, :]
    cos, sin = jnp.cos(ang), jnp.sin(ang)
    reps = LANES // HEAD_DIM
    cos_t = jnp.tile(jnp.concatenate([cos, cos], axis=-1), (1, reps))
    sin_t = jnp.tile(jnp.concatenate([-sin, sin], axis=-1), (1, reps))
    return cos_t, sin_t


def _pad_rows(w, rows):
    return jnp.pad(w, ((0, rows - w.shape[0]), (0, 0)))


def kernel(x, g_mix, w_in, w_dw, b_dw, ln_g, ln_b, lambda_q1, lambda_k1, lambda_q2, lambda_k2,
           g_subln, w_out, g_ffn, w_up, w_fconv, b_fconv, w_down, g_final):
    depth = w_in.shape[0]
    seq = x.shape[1]
    cos_t, sin_t = _rope_tables(seq)
    bf = jnp.bfloat16
    row = lambda v: v.reshape(1, -1).astype(jnp.float32)
    for layer in range(depth):
        w_main = w_in[layer][:, :QK_COLS].astype(bf)
        w_vt = w_in[layer][:, QK_COLS:].T.astype(bf)
        a_pre, q, k, vt = _mix_in(x, row(g_mix[layer]), w_main, w_vt, cos_t, sin_t)
        w_taps = jnp.broadcast_to(w_dw[layer].astype(jnp.float32)[:, None, :],
                                  (CONV_KERNEL, SUBLANES, CONV_WIDTH))
        a = _conv_mod(a_pre, w_taps, row(b_dw[layer]), row(ln_g[layer]), row(ln_b[layer]))
        lam_init = 0.8 - 0.6 * math.exp(-0.3 * layer)
        lam_rows = jnp.stack([lambda_q1[layer], lambda_k1[layer], lambda_q2[layer], lambda_k2[layer]])
        lamv = jnp.pad(lam_rows.astype(jnp.float32),
                       ((0, SUBLANES - 4), (0, LANES - lam_rows.shape[1])))
        o = _diff_attn(q, k, vt, lamv, row(g_subln[layer]), lam_init)
        x = _ffn_out(x, a, o, w_out[layer].astype(bf), row(g_ffn[layer]), w_up[layer].astype(bf),
                     _pad_rows(w_fconv[layer], SUBLANES), row(b_fconv[layer]),
                     w_down[layer].astype(bf), row(g_final), final_norm=(layer == depth - 1))
    return x
```
